```python
import math
import jax, jax.numpy as jnp
from jax import lax
import numpy as np

D_MODEL = 1024
BATCH = 16
SEQ = 2048
DEPTH = 1
DEC_BATCH = 128
DEC_SEQ = 8
PAST_LEN = 8192
PAGE_SIZE = 128

DA_HEADS = 8
DA_QK_DIM = 64
DA_V_DIM = 2 * DA_QK_DIM
DA_QK_WIDTH = DA_HEADS * 2 * DA_QK_DIM
DA_V_WIDTH = DA_HEADS * DA_V_DIM
GM_GROUPS = 4
GM_CHUNK = 128
GM_GROUP_DIM = 128
GM_WIDTH = GM_GROUPS * GM_GROUP_DIM
N_BRANCH = 2
IN_WIDTH = 2 * DA_QK_WIDTH + DA_V_WIDTH + 2 * GM_WIDTH + N_BRANCH * D_MODEL
D_FF = 4 * D_MODEL
PLE_DIM = 256
Q_BLOCK = 128
EPS = 1e-6
SUBLN_EPS = 1e-5
NEG_INF = -1e30

kernel_name = "diffattn_gmlp_gated_hybrid_step"


def rmsnorm(x, g, eps=EPS):
    xf = x.astype(jnp.float32)
    y = xf * lax.rsqrt(jnp.mean(xf * xf, axis=-1, keepdims=True) + eps)
    return (y * g.astype(jnp.float32)).astype(x.dtype)


def layernorm(x, g, b, eps=EPS):
    xf = x.astype(jnp.float32)
    mu = jnp.mean(xf, axis=-1, keepdims=True)
    xc = xf - mu
    y = xc * lax.rsqrt(jnp.mean(xc * xc, axis=-1, keepdims=True) + eps)
    return (y * g.astype(jnp.float32) + b.astype(jnp.float32)).astype(x.dtype)


def lambda_init_for(layer_idx):
    return 0.8 - 0.6 * math.exp(-0.3 * layer_idx)


def diff_lambda(lq1, lk1, lq2, lk2, lam0):
    f = jnp.float32
    return (jnp.exp(jnp.sum(lq1.astype(f) * lk1.astype(f)))
            - jnp.exp(jnp.sum(lq2.astype(f) * lk2.astype(f))) + lam0)


def diff_weights(s, lam):
    p = jax.nn.softmax(s, axis=-1)
    return jnp.take(p, 0, axis=-3) - lam * jnp.take(p, 1, axis=-3)


def mixer_inputs(x, pre_g, w_in):
    lead = x.shape[:-1]
    h = rmsnorm(x, pre_g)
    z = h @ w_in
    i1 = DA_QK_WIDTH
    i2 = 2 * DA_QK_WIDTH
    i3 = i2 + DA_V_WIDTH
    i4 = i3 + GM_WIDTH
    i5 = i4 + GM_WIDTH
    q, k, v, u, gv, gates = jnp.split(z, [i1, i2, i3, i4, i5], axis=-1)
    q = q.reshape(*lead, DA_HEADS, 2, DA_QK_DIM)
    k = k.reshape(*lead, DA_HEADS, 2, DA_QK_DIM)
    v = v.reshape(*lead, DA_HEADS, DA_V_DIM)
    u = jax.nn.gelu(u)
    gv = jax.nn.gelu(gv)
    gates = jax.nn.sigmoid(gates.astype(jnp.float32)).astype(x.dtype).reshape(*lead, N_BRANCH, D_MODEL)
    return q, k, v, u, gv, gates


def diff_attn_prompt(q, k, v, lam):
    B, S = q.shape[0], q.shape[1]
    nqb = S // Q_BLOCK
    scale = DA_QK_DIM ** -0.5
    qb = jnp.moveaxis(q.reshape(B, nqb, Q_BLOCK, DA_HEADS, 2, DA_QK_DIM), 1, 0)
    kpos = jnp.arange(S)

    def block(args):
        qblk, bi = args
        s = jnp.einsum('bqhmd,bkhmd->bhmqk', qblk, k, preferred_element_type=jnp.float32) * scale
        qpos = bi * Q_BLOCK + jnp.arange(Q_BLOCK)
        s = jnp.where(kpos[None, :] <= qpos[:, None], s, NEG_INF)
        w = diff_weights(s, lam)
        return jnp.einsum('bhqk,bkhe->bqhe', w.astype(v.dtype), v)

    o = lax.map(block, (qb, jnp.arange(nqb)))
    return jnp.moveaxis(o, 0, 1).reshape(B, S, DA_HEADS, DA_V_DIM)


def diff_attn_sample(q, k_new, v_new, cache_k, cache_v, layer_idx, page_table, lam):
    T = q.shape[1]
    n_pages = page_table.shape[1]
    past = n_pages * PAGE_SIZE
    scale = DA_QK_DIM ** -0.5
    kpos = jnp.arange(past + T)
    qpos = past + jnp.arange(T)
    mask = kpos[None, :] <= qpos[:, None]

    def one(args):
        qi, kn, vn, pt = args
        kp = cache_k[layer_idx, pt].reshape(past, DA_HEADS, 2, DA_QK_DIM)
        vp = cache_v[layer_idx, pt].reshape(past, DA_HEADS, DA_V_DIM)
        ka = jnp.concatenate([kp, kn.astype(kp.dtype)], axis=0)
        va = jnp.concatenate([vp, vn.astype(vp.dtype)], axis=0)
        s = jnp.einsum('qhmd,khmd->hmqk', qi, ka, preferred_element_type=jnp.float32) * scale
        s = jnp.where(mask, s, NEG_INF)
        w = diff_weights(s, lam)
        return jnp.einsum('hqk,khe->qhe', w.astype(va.dtype), va).astype(qi.dtype)

    return lax.map(one, (q, k_new, v_new, page_table))


def gmlp_branch(u, gv, gm_g, gm_b, ws, bs):
    N, T = u.shape[0], u.shape[1]
    vn = layernorm(gv, gm_g, gm_b)
    L = min(T, GM_CHUNK)
    nc = T // L
    wm = jnp.tril(ws)[:, :L, :L]
    vc = vn.reshape(N, nc, L, GM_GROUPS, GM_GROUP_DIM)
    mixed = jnp.einsum('gts,ncsgd->nctgd', wm, vc) + jnp.transpose(bs[:, :L])[:, :, None]
    return u * mixed.reshape(N, T, GM_WIDTH), vn


def layer_tail(x, o_attn, o_gm, gates, subln_g, lam0, w_branch_attn, w_branch_gm, w_o,
               post_mix_g, pre_ffn_g, w_up, w_down, post_ffn_g, pe, w_pe, w_pg, pe_post_g):
    lead = x.shape[:-1]
    oa = (rmsnorm(o_attn, subln_g, SUBLN_EPS) * (1.0 - lam0)).reshape(*lead, DA_V_WIDTH)
    merged = gates[..., 0, :] * (oa @ w_branch_attn) + gates[..., 1, :] * (o_gm @ w_branch_gm)
    x = x + rmsnorm(merged @ w_o, post_mix_g)
    h = rmsnorm(x, pre_ffn_g)
    f = jnp.square(jax.nn.relu(h @ w_up)) @ w_down
    x = x + rmsnorm(f, post_ffn_g)
    gate = jax.nn.sigmoid((x @ w_pg).astype(jnp.float32)).astype(x.dtype)
    x = x + rmsnorm(gate * (pe @ w_pe), pe_post_g)
    return x


def setup_inputs(seed: int = 0) -> dict:
    key = jax.random.key(seed)
    ks = jax.random.split(key, 40)
    f = jnp.float32
    n_pages = PAST_LEN // PAGE_SIZE
    n_used = DEC_BATCH * n_pages
    n_pool = n_used + max(1, n_used // 4)

    def nrm(k, shape, scale):
        return jax.random.normal(k, shape, f) * scale

    def gain(k, shape):
        return 1.0 + 0.05 * jax.random.normal(k, shape, f)

    page_table = jax.random.permutation(ks[0], n_pool)[:n_used].reshape(DEC_BATCH, n_pages).astype(jnp.int32)
    return {
        'x_prompt': nrm(ks[1], (BATCH, SEQ, D_MODEL), 1.0),
        'x_sample': nrm(ks[2], (DEC_BATCH, DEC_SEQ, D_MODEL), 1.0),
        'cache_k': nrm(ks[3], (DEPTH, n_pool, PAGE_SIZE, DA_HEADS, 2, DA_QK_DIM), 1.0),
        'cache_v': nrm(ks[4], (DEPTH, n_pool, PAGE_SIZE, DA_HEADS, DA_V_DIM), 1.0),
        'page_table': page_table,
        'p_prompt': nrm(ks[5], (DEPTH, BATCH, SEQ, PLE_DIM), 1.0),
        'p_sample': nrm(ks[6], (DEPTH, DEC_BATCH, DEC_SEQ, PLE_DIM), 1.0),
        'pre_mix_g': gain(ks[7], (DEPTH, D_MODEL)),
        'w_in': nrm(ks[8], (DEPTH, D_MODEL, IN_WIDTH), D_MODEL ** -0.5),
        'lam_q1': nrm(ks[9], (DEPTH, DA_QK_DIM), 0.1),
        'lam_k1': nrm(ks[10], (DEPTH, DA_QK_DIM), 0.1),
        'lam_q2': nrm(ks[11], (DEPTH, DA_QK_DIM), 0.1),
        'lam_k2': nrm(ks[12], (DEPTH, DA_QK_DIM), 0.1),
        'subln_g': gain(ks[13], (DEPTH, DA_V_DIM)),
        'gm_norm_g': gain(ks[14], (DEPTH, GM_WIDTH)),
        'gm_norm_b': nrm(ks[15], (DEPTH, GM_WIDTH), 0.02),
        'gm_ws': nrm(ks[16], (DEPTH, GM_GROUPS, GM_CHUNK, GM_CHUNK), 0.5 * GM_CHUNK ** -0.5),
        'gm_bs': gain(ks[17], (DEPTH, GM_GROUPS, GM_CHUNK)),
        'w_branch_attn': nrm(ks[18], (DEPTH, DA_V_WIDTH, D_MODEL), DA_V_WIDTH ** -0.5),
        'w_branch_gm': nrm(ks[19], (DEPTH, GM_WIDTH, D_MODEL), GM_WIDTH ** -0.5),
        'w_o': nrm(ks[20], (DEPTH, D_MODEL, D_MODEL), D_MODEL ** -0.5),
        'post_mix_g': gain(ks[21], (DEPTH, D_MODEL)),
        'pre_ffn_g': gain(ks[22], (DEPTH, D_MODEL)),
        'w_up': nrm(ks[23], (DEPTH, D_MODEL, D_FF), D_MODEL ** -0.5),
        'w_down': nrm(ks[24], (DEPTH, D_FF, D_MODEL), D_FF ** -0.5),
        'post_ffn_g': gain(ks[25], (DEPTH, D_MODEL)),
        'w_pe': nrm(ks[26], (DEPTH, PLE_DIM, D_MODEL), PLE_DIM ** -0.5),
        'w_pg': nrm(ks[27], (DEPTH, D_MODEL, D_MODEL), D_MODEL ** -0.5),
        'pe_post_g': gain(ks[28], (DEPTH, D_MODEL)),
    }


def reference(x_prompt, x_sample, cache_k, cache_v, page_table, p_prompt, p_sample,
              pre_mix_g, w_in, lam_q1, lam_k1, lam_q2, lam_k2, subln_g, gm_norm_g, gm_norm_b,
              gm_ws, gm_bs, w_branch_attn, w_branch_gm, w_o, post_mix_g, pre_ffn_g, w_up, w_down,
              post_ffn_g, w_pe, w_pg, pe_post_g):
    xp, xs = x_prompt, x_sample
    kp_list, vp_list, gvp_list = [], [], []
    ks_list, vs_list, gvs_list = [], [], []
    for i in range(DEPTH):
        lam0 = lambda_init_for(i)
        lam = diff_lambda(lam_q1[i], lam_k1[i], lam_q2[i], lam_k2[i], lam0)
        tail = dict(subln_g=subln_g[i], lam0=lam0, w_branch_attn=w_branch_attn[i],
                    w_branch_gm=w_branch_gm[i], w_o=w_o[i], post_mix_g=post_mix_g[i],
                    pre_ffn_g=pre_ffn_g[i], w_up=w_up[i], w_down=w_down[i], post_ffn_g=post_ffn_g[i],
                    w_pe=w_pe[i], w_pg=w_pg[i], pe_post_g=pe_post_g[i])

        q, k, v, u, gv, gates = mixer_inputs(xp, pre_mix_g[i], w_in[i])
        o_attn = diff_attn_prompt(q, k, v, lam)
        o_gm, gvn = gmlp_branch(u, gv, gm_norm_g[i], gm_norm_b[i], gm_ws[i], gm_bs[i])
        xp = layer_tail(xp, o_attn, o_gm, gates, pe=p_prompt[i], **tail)
        kp_list.append(k)
        vp_list.append(v)
        gvp_list.append(gvn)

        q, k, v, u, gv, gates = mixer_inputs(xs, pre_mix_g[i], w_in[i])
        o_attn = diff_attn_sample(q, k, v, cache_k, cache_v, i, page_table, lam)
        o_gm, gvn = gmlp_branch(u, gv, gm_norm_g[i], gm_norm_b[i], gm_ws[i], gm_bs[i])
        xs = layer_tail(xs, o_attn, o_gm, gates, pe=p_sample[i], **tail)
        ks_list.append(k)
        vs_list.append(v)
        gvs_list.append(gvn)

    return (xp, xs, jnp.stack(kp_list), jnp.stack(vp_list), jnp.stack(gvp_list),
            jnp.stack(ks_list), jnp.stack(vs_list), jnp.stack(gvs_list))
```

```python
import functools
import math

import jax
import jax.numpy as jnp
from jax import lax
from jax.experimental import pallas as pl
from jax.experimental.pallas import tpu as pltpu

F32 = jnp.float32
BF16 = jnp.bfloat16

D_MODEL = 1024
N_HEADS = 8
QK_DIM = 64
HEAD_DIM = 2 * QK_DIM
QK_WIDTH = N_HEADS * HEAD_DIM
V_WIDTH = N_HEADS * HEAD_DIM
GM_GROUPS = 4
GM_CHUNK = 128
GM_GROUP_DIM = 128
GM_WIDTH = GM_GROUPS * GM_GROUP_DIM
D_FF = 4 * D_MODEL
PLE_DIM = 256
PAGE = 128
EPS = 1e-6
SUBLN_EPS = 1e-5
NEG_INF = -1e30
QK_SCALE = QK_DIM ** -0.5

VMEM_LIMIT_BYTES = 56 * 1024 * 1024
LANES = 128

_Q0, _K0, _V0 = 0, QK_WIDTH, 2 * QK_WIDTH
_U0 = _V0 + V_WIDTH
_GV0 = _U0 + GM_WIDTH
_G0 = _GV0 + GM_WIDTH
IN_WIDTH = _G0 + 2 * D_MODEL


def _lambda_init(layer_idx):
    return 0.8 - 0.6 * math.exp(-0.3 * layer_idx)


def _rms(x, g, eps):
    return x * lax.rsqrt(jnp.mean(x * x, axis=-1, keepdims=True) + eps) * g


def _nt_dot(a, b):
    return lax.dot_general(a, b, (((1,), (1,)), ((), ())), preferred_element_type=F32)


def _resident(shape):
    nd = len(shape)
    return pl.BlockSpec(shape, lambda *_: (0,) * nd, pipeline_mode=pl.Buffered(1))


def _inproj_kernel(*refs, chunk_len, native_kv):
    if native_kv:
        (x_ref, g_ref, w_ref, wkt_ref, gmg_ref, gmb_ref, ws_ref, bias_ref,
         q_ref, kt_ref, ktb_ref, v_ref, vb_ref, gvn_ref, ogm_ref, gates_ref) = refs
    else:
        (x_ref, g_ref, w_ref, gmg_ref, gmb_ref, ws_ref, bias_ref,
         q_ref, k_ref, v_ref, gvn_ref, ogm_ref, gates_ref) = refs
    tm = x_ref.shape[0]
    h = _rms(x_ref[...], g_ref[...], EPS).astype(BF16)

    def seg(a, b):
        return jnp.dot(h, w_ref[:, a:b], preferred_element_type=F32)

    q_ref[...] = (seg(_Q0, _K0) * QK_SCALE).astype(q_ref.dtype)
    zv = seg(_V0, _U0)
    if native_kv:
        zkt = _nt_dot(wkt_ref[...], h)
        kt_ref[0] = zkt
        ktb_ref[0] = zkt.astype(BF16)
        for hd in range(N_HEADS):
            v_ref[pl.ds(hd, tm, stride=N_HEADS), :] = zv[:, hd * HEAD_DIM:(hd + 1) * HEAD_DIM]
        vb_ref[...] = zv.astype(BF16)
    else:
        k_ref[...] = seg(_K0, _V0)
        v_ref[...] = zv
    gates_ref[...] = jax.nn.sigmoid(seg(_G0, IN_WIDTH))

    u = jax.nn.gelu(seg(_U0, _GV0))
    gv = jax.nn.gelu(seg(_GV0, _G0))
    mu = jnp.mean(gv, axis=-1, keepdims=True)
    xc = gv - mu
    vn = xc * lax.rsqrt(jnp.mean(xc * xc, axis=-1, keepdims=True) + EPS) * gmg_ref[...] + gmb_ref[...]
    gvn_ref[...] = vn
    vnb = vn.astype(BF16)

    row = lax.broadcasted_iota(jnp.int32, (GM_CHUNK, GM_CHUNK), 0)
    col = lax.broadcasted_iota(jnp.int32, (GM_CHUNK, GM_CHUNK), 1)
    sh = chunk_len.bit_length() - 1
    keep = (col <= row) & ((row >> sh) == (col >> sh))
    for g in range(GM_GROUPS):
        wm = jnp.where(keep, ws_ref[g], 0.0).astype(BF16)
        cs = slice(g * GM_GROUP_DIM, (g + 1) * GM_GROUP_DIM)
        for c in range(tm // GM_CHUNK):
            rs = slice(c * GM_CHUNK, (c + 1) * GM_CHUNK)
            mixed = jnp.dot(wm, vnb[rs, cs], preferred_element_type=F32) + bias_ref[:, cs]
            ogm_ref[rs, cs] = (u[rs, cs] * mixed).astype(BF16)


def _inproj(x, pre_g, w_in_b, wkt_b, gm_g, gm_b, ws_blk, bias_blk, *, chunk_len, seq_len, tm):
    m = x.shape[0]
    native_kv = wkt_b is not None
    assert m % tm == 0 and tm % GM_CHUNK == 0
    assert chunk_len & (chunk_len - 1) == 0 and GM_CHUNK % chunk_len == 0
    row = lambda w: pl.BlockSpec((tm, w), lambda i: (i, 0))
    sds = jax.ShapeDtypeStruct
    tail_shape = [sds((m, GM_WIDTH), F32), sds((m, GM_WIDTH), BF16), sds((m, 2 * D_MODEL), F32)]
    tail_specs = [row(GM_WIDTH), row(GM_WIDTH), row(2 * D_MODEL)]
    consts = [_resident((1, GM_WIDTH)), _resident((1, GM_WIDTH)),
              _resident((GM_GROUPS, GM_CHUNK, GM_CHUNK)), _resident((GM_CHUNK, GM_WIDTH))]
    if native_kv:
        assert seq_len % tm == 0
        spb = seq_len // tm
        nb = m // seq_len
        kt_spec = pl.BlockSpec((1, QK_WIDTH, tm), lambda i: (i // spb, 0, i % spb))
        out_shape = [sds((m, QK_WIDTH), BF16), sds((nb, QK_WIDTH, seq_len), F32), sds((nb, QK_WIDTH, seq_len), BF16),
                     sds((m * N_HEADS, HEAD_DIM), F32), sds((m, V_WIDTH), BF16)] + tail_shape
        out_specs = [row(QK_WIDTH), kt_spec, kt_spec,
                     pl.BlockSpec((tm * N_HEADS, HEAD_DIM), lambda i: (i, 0)), row(V_WIDTH)] + tail_specs
        in_specs = [row(D_MODEL), _resident((1, D_MODEL)), _resident((D_MODEL, IN_WIDTH)),
                    _resident((QK_WIDTH, D_MODEL))] + consts
        args = (x, pre_g, w_in_b, wkt_b, gm_g, gm_b, ws_blk, bias_blk)
    else:
        out_shape = [sds((m, QK_WIDTH), F32), sds((m, QK_WIDTH), F32), sds((m, V_WIDTH), F32)] + tail_shape
        out_specs = [row(QK_WIDTH), row(QK_WIDTH), row(V_WIDTH)] + tail_specs
        in_specs = [row(D_MODEL), _resident((1, D_MODEL)), _resident((D_MODEL, IN_WIDTH))] + consts
        args = (x, pre_g, w_in_b, gm_g, gm_b, ws_blk, bias_blk)
    return pl.pallas_call(
        functools.partial(_inproj_kernel, chunk_len=chunk_len, native_kv=native_kv),
        grid=(m // tm,),
        in_specs=in_specs,
        out_specs=out_specs,
        out_shape=out_shape,
        compiler_params=pltpu.CompilerParams(dimension_semantics=("arbitrary",),
                                             vmem_limit_bytes=VMEM_LIMIT_BYTES),
        name="inproj",
    )(*args)


def _diff_lambda(lamv_ref, lam0):
    lv = lamv_ref[...]
    a = jnp.sum(lv[0:1] * lv[1:2], axis=-1, keepdims=True)
    b = jnp.sum(lv[2:3] * lv[3:4], axis=-1, keepdims=True)
    return jnp.exp(a) - jnp.exp(b) + lam0


def _subln(o, sg, lam0):
    return _rms(o, sg, SUBLN_EPS) * (1.0 - lam0)


def _prompt_attn_kernel(lamv_ref, q_ref, kt_ref, v_ref, sg_ref, o_ref, s_scr, m_scr, l_scr, acc_scr,
                        *, tq, lam0):
    qi = pl.program_id(2)
    lam = _diff_lambda(lamv_ref, lam0)
    q = q_ref[0]
    lane = lax.broadcasted_iota(jnp.int32, q.shape, 1)
    zero = jnp.zeros_like(q)
    qq = jnp.concatenate([jnp.where(lane < QK_DIM, q, zero), jnp.where(lane >= QK_DIM, q, zero)], axis=0)

    def scores(j):
        ktj = kt_ref[0, :, pl.ds(pl.multiple_of(j * tq, tq), tq)]
        return jnp.dot(qq, ktj, preferred_element_type=F32)

    def fold_max(s):
        m = m_scr[...]
        for c in range(tq // LANES):
            m = jnp.maximum(m, s[:, c * LANES:(c + 1) * LANES])
        m_scr[...] = m

    m_scr[...] = jnp.full(m_scr.shape, NEG_INF, F32)

    def pass_a(j, carry):
        s = scores(j)
        s_scr[j] = s
        fold_max(s)
        return carry

    lax.fori_loop(0, qi, pass_a, 0)
    s = scores(qi)
    r = lax.broadcasted_iota(jnp.int32, (2 * tq, tq), 0)
    c = lax.broadcasted_iota(jnp.int32, (2 * tq, tq), 1)
    r = jnp.where(r >= tq, r - tq, r)
    s = jnp.where(c <= r, s, NEG_INF)
    s_scr[qi] = s
    fold_max(s)

    m = jnp.max(m_scr[...], axis=-1, keepdims=True)
    l_scr[...] = jnp.zeros(l_scr.shape, F32)
    acc_scr[...] = jnp.zeros(acc_scr.shape, F32)

    def pass_b(j, carry):
        e = jnp.exp(s_scr[j] - m)
        l = l_scr[...]
        for cc in range(tq // LANES):
            l = l + e[:, cc * LANES:(cc + 1) * LANES]
        l_scr[...] = l
        vj = v_ref[0, pl.ds(pl.multiple_of(j * tq, tq), tq), :]
        acc_scr[...] += jnp.dot(e.astype(BF16), vj, preferred_element_type=F32)
        return carry

    lax.fori_loop(0, qi + 1, pass_b, 0)
    l = jnp.sum(l_scr[...], axis=-1, keepdims=True)
    acc = acc_scr[...]
    o = acc[:tq] / l[:tq] - lam * (acc[tq:] / l[tq:])
    o_ref[0] = _subln(o, sg_ref[...], lam0).astype(o_ref.dtype)


def _prompt_attn(lamv, qb, ktb, vb, subln_g, *, lam0, tq):
    b, s, _ = qb.shape
    nq = s // tq
    return pl.pallas_call(
        functools.partial(_prompt_attn_kernel, tq=tq, lam0=lam0),
        grid=(b, N_HEADS, nq),
        in_specs=[pl.BlockSpec((4, QK_DIM), lambda bi, h, qi: (0, 0)),
                  pl.BlockSpec((1, tq, HEAD_DIM), lambda bi, h, qi: (bi, qi, h)),
                  pl.BlockSpec((1, HEAD_DIM, s), lambda bi, h, qi: (bi, h, 0)),
                  pl.BlockSpec((1, s, HEAD_DIM), lambda bi, h, qi: (bi, 0, h)),
                  pl.BlockSpec((1, HEAD_DIM), lambda bi, h, qi: (0, 0))],
        out_specs=pl.BlockSpec((1, tq, HEAD_DIM), lambda bi, h, qi: (bi, qi, h)),
        out_shape=jax.ShapeDtypeStruct((b, s, V_WIDTH), BF16),
        scratch_shapes=[pltpu.VMEM((nq, 2 * tq, tq), F32),
                        pltpu.VMEM((2 * tq, LANES), F32),
                        pltpu.VMEM((2 * tq, LANES), F32),
                        pltpu.VMEM((2 * tq, HEAD_DIM), F32)],
        compiler_params=pltpu.CompilerParams(dimension_semantics=("arbitrary",) * 3,
                                             vmem_limit_bytes=VMEM_LIMIT_BYTES),
        name="prompt_attn",
    )(lamv, qb, ktb, vb, subln_g)


PAGES_PER_STEP = 8
HM_ROWS = N_HEADS * 2 * 8


def _sample_attn_kernel(pt_ref, lamv_ref, q_ref, kn_ref, vn_ref, sg_ref, *refs, n_pages, t_new, lam0):
    del pt_ref
    P = PAGES_PER_STEP
    k_refs, v_refs = refs[:P], refs[P:2 * P]
    o_ref, s_scr, w_scr, acc_scr = refs[2 * P:]
    t = pl.program_id(1)
    k_steps = n_pages // P
    past = n_pages * PAGE
    rows = N_HEADS * 2 * t_new
    tsh = t_new.bit_length() - 1
    qsh = QK_DIM.bit_length() - 1

    def q_blockdiag():
        q = q_ref[0]
        qt = jnp.concatenate([q] * (rows // t_new), axis=0)
        r = lax.broadcasted_iota(jnp.int32, qt.shape, 0)
        c = lax.broadcasted_iota(jnp.int32, qt.shape, 1)
        return jnp.where((c >> qsh) == (r >> tsh), qt, 0.0).astype(BF16)

    def pad_rows(a):
        return jnp.concatenate([a, jnp.zeros((PAGE - t_new, a.shape[1]), F32)], axis=0).astype(BF16)

    @pl.when(t < k_steps)
    def _():
        qbd = q_blockdiag()
        for i in range(0, P, 2):
            kt2 = jnp.concatenate([k_refs[i][0], k_refs[i + 1][0]], axis=1).astype(BF16)
            s = jnp.dot(qbd, kt2, preferred_element_type=F32)
            off = pl.multiple_of((t * P + i) * PAGE, 2 * PAGE)
            s_scr[:, pl.ds(off, 2 * PAGE)] = s

    @pl.when(t == k_steps - 1)
    def _():
        lam = _diff_lambda(lamv_ref, lam0)
        qbd = q_blockdiag()
        s = _nt_dot(qbd, pad_rows(kn_ref[0]))
        r = lax.broadcasted_iota(jnp.int32, s.shape, 0)
        c = lax.broadcasted_iota(jnp.int32, s.shape, 1)
        s_scr[:, past:past + PAGE] = jnp.where(c <= (r & (t_new - 1)), s, NEG_INF)

        n_chunks = (past + PAGE) // PAGE

        def max_body(j, m):
            return jnp.maximum(m, s_scr[:, pl.ds(pl.multiple_of(j * PAGE, PAGE), PAGE)])

        m = lax.fori_loop(0, n_chunks, max_body, jnp.full((rows, PAGE), NEG_INF, F32))
        m = jnp.max(m, axis=-1, keepdims=True)

        def exp_body(j, l):
            sl = pl.ds(pl.multiple_of(j * PAGE, PAGE), PAGE)
            e = jnp.exp(s_scr[:, sl] - m)
            s_scr[:, sl] = e
            return l + e

        l = lax.fori_loop(0, n_chunks, exp_body, jnp.zeros((rows, PAGE), F32))
        inv = 1.0 / jnp.sum(l, axis=-1, keepdims=True)
        r1 = lax.broadcasted_iota(jnp.int32, inv.shape, 0)
        coef = jnp.where(((r1 >> tsh) & 1) == 0, inv, lam * inv)

        def w_body(j, carry):
            sl = pl.ds(pl.multiple_of(j * PAGE, PAGE), PAGE)
            p = s_scr[:, sl] * coef
            w = jnp.concatenate(
                [p[h * 2 * t_new:h * 2 * t_new + t_new] - p[h * 2 * t_new + t_new:(h + 1) * 2 * t_new]
                 for h in range(N_HEADS)], axis=0)
            w_scr[:, sl] = w.astype(BF16)
            return carry

        lax.fori_loop(0, n_chunks, w_body, 0)
        acc_scr[...] = jnp.zeros(acc_scr.shape, F32)

    def pv(w_cols, head_rows):
        outs = []
        for cb in range(N_HEADS // 2):
            w2 = w_scr[cb * 2 * t_new:(cb + 1) * 2 * t_new, w_cols]
            v2 = jnp.concatenate([head_rows(2 * cb), head_rows(2 * cb + 1)], axis=1).astype(BF16)
            r2 = jnp.dot(w2, v2, preferred_element_type=F32)
            outs += [r2[:t_new, :HEAD_DIM], r2[t_new:, HEAD_DIM:]]
        return jnp.concatenate(outs, axis=1)

    @pl.when(t >= k_steps)
    def _():
        acc = acc_scr[...]
        for i in range(P):
            off = pl.multiple_of(((t - k_steps) * P + i) * PAGE, PAGE)
            v_ref = v_refs[i]
            acc = acc + pv(pl.ds(off, PAGE), lambda hd: v_ref[0, pl.ds(hd, PAGE, stride=N_HEADS), :])
        acc_scr[...] = acc

    @pl.when(t == 2 * k_steps - 1)
    def _():
        vn = jnp.concatenate([vn_ref[0], jnp.zeros((PAGE - t_new, V_WIDTH), F32)], axis=0)
        o = acc_scr[...] + pv(slice(past, past + PAGE), lambda hd: vn[:, hd * HEAD_DIM:(hd + 1) * HEAD_DIM])
        sg = sg_ref[...]
        o_ref[0] = jnp.concatenate(
            [_subln(o[:, h * HEAD_DIM:(h + 1) * HEAD_DIM], sg, lam0) for h in range(N_HEADS)], axis=1)


def _sample_attn(page_table, lamv, q, k_new, v_new, subln_g, cache_kt, cache_vr, *, page0, lam0):
    n, t_new, _ = q.shape
    n_pages = page_table.shape[1]
    P = PAGES_PER_STEP
    assert n_pages % P == 0 and P % 2 == 0 and N_HEADS * 2 * t_new == HM_ROWS
    k_steps = n_pages // P
    tok = pl.BlockSpec((1, t_new, QK_WIDTH), lambda ni, t, pt: (ni, 0, 0))

    def k_spec(i):
        return pl.BlockSpec((1, QK_WIDTH, PAGE),
                            lambda ni, t, pt: (page0 + pt[ni, jnp.minimum(t, k_steps - 1) * P + i], 0, 0))

    def v_spec(i):
        return pl.BlockSpec((1, PAGE * N_HEADS, HEAD_DIM),
                            lambda ni, t, pt: (page0 + pt[ni, jnp.maximum(t - k_steps, 0) * P + i], 0, 0))

    grid_spec = pltpu.PrefetchScalarGridSpec(
        num_scalar_prefetch=1,
        grid=(n, 2 * k_steps),
        in_specs=[pl.BlockSpec((4, QK_DIM), lambda ni, t, pt: (0, 0)), tok, tok, tok,
                  pl.BlockSpec((1, HEAD_DIM), lambda ni, t, pt: (0, 0))]
                 + [k_spec(i) for i in range(P)] + [v_spec(i) for i in range(P)],
        out_specs=pl.BlockSpec((1, t_new, V_WIDTH), lambda ni, t, pt: (ni, 0, 0)),
        scratch_shapes=[pltpu.VMEM((HM_ROWS, n_pages * PAGE + PAGE), F32),
                        pltpu.VMEM((N_HEADS * t_new, n_pages * PAGE + PAGE), BF16),
                        pltpu.VMEM((t_new, V_WIDTH), F32)],
    )
    return pl.pallas_call(
        functools.partial(_sample_attn_kernel, n_pages=n_pages, t_new=t_new, lam0=lam0),
        grid_spec=grid_spec,
        out_shape=jax.ShapeDtypeStruct((n, t_new, V_WIDTH), F32),
        compiler_params=pltpu.CompilerParams(dimension_semantics=("arbitrary", "arbitrary"),
                                             vmem_limit_bytes=VMEM_LIMIT_BYTES),
        name="sample_attn",
    )(page_table, lamv, q, k_new, v_new, subln_g, *([cache_kt] * P), *([cache_vr] * P))


def _tail_kernel(x_ref, oa_ref, ogm_ref, gates_ref, pe_ref,
                 wba_ref, wbg_ref, wo_ref, wup_ref, wdn_ref, wpg_ref, wpe_ref,
                 g_mix_ref, g_pre_ref, g_ffn_ref, g_pe_ref, y_ref):
    dot = functools.partial(jnp.dot, preferred_element_type=F32)
    gates = gates_ref[...]
    merged = (gates[:, :D_MODEL] * dot(oa_ref[...].astype(BF16), wba_ref[...])
              + gates[:, D_MODEL:] * dot(ogm_ref[...], wbg_ref[...]))
    x = x_ref[...] + _rms(dot(merged.astype(BF16), wo_ref[...]), g_mix_ref[...], EPS)
    h = _rms(x, g_pre_ref[...], EPS).astype(BF16)
    up = jnp.maximum(dot(h, wup_ref[...]), 0.0)
    f = dot((up * up).astype(BF16), wdn_ref[...])
    x = x + _rms(f, g_ffn_ref[...], EPS)
    gate = jax.nn.sigmoid(dot(x.astype(BF16), wpg_ref[...]))
    x = x + _rms(gate * dot(pe_ref[...].astype(BF16), wpe_ref[...]), g_pe_ref[...], EPS)
    y_ref[...] = x


def _tail(x, oa, ogm, gates, pe, wts, gains, *, tm):
    m = x.shape[0]
    assert m % tm == 0
    row = lambda w: pl.BlockSpec((tm, w), lambda i: (i, 0))
    return pl.pallas_call(
        _tail_kernel,
        grid=(m // tm,),
        in_specs=[row(D_MODEL), row(V_WIDTH), row(GM_WIDTH), row(2 * D_MODEL), row(PLE_DIM)]
                 + [_resident(w.shape) for w in wts] + [_resident(g.shape) for g in gains],
        out_specs=row(D_MODEL),
        out_shape=jax.ShapeDtypeStruct((m, D_MODEL), F32),
        compiler_params=pltpu.CompilerParams(dimension_semantics=("arbitrary",),
                                             vmem_limit_bytes=VMEM_LIMIT_BYTES),
        name="tail",
    )(x, oa, ogm, gates, pe, *wts, *gains)


def _gmlp_blocks(ws, bs, chunk_len):
    rep = GM_CHUNK // chunk_len
    ws_blk = jnp.tile(ws[:, :chunk_len, :chunk_len], (1, rep, rep))
    bias = jnp.tile(jnp.transpose(bs[:, :chunk_len]), (rep, 1))
    return ws_blk, jnp.repeat(bias, GM_GROUP_DIM, axis=1)


def kernel(x_prompt, x_sample, cache_k, cache_v, page_table, p_prompt, p_sample, pre_mix_g, w_in, lam_q1, lam_k1, lam_q2, lam_k2, subln_g, gm_norm_g, gm_norm_b, gm_ws, gm_bs, w_branch_attn, w_branch_gm, w_o, post_mix_g, pre_ffn_g, w_up, w_down, post_ffn_g, w_pe, w_pg, pe_post_g):
    depth, n_pool = cache_k.shape[0], cache_k.shape[1]
    assert depth == 1
    b, s, _ = x_prompt.shape
    n, t_new, _ = x_sample.shape
    i = 0
    lam0 = _lambda_init(i)
    row2 = lambda a: a[i].reshape(1, -1)
    w_in_b = w_in[i].astype(BF16)
    wkt_b = jnp.transpose(w_in[i][:, _K0:_V0]).astype(BF16)
    wts = tuple(w[i].astype(BF16) for w in (w_branch_attn, w_branch_gm, w_o, w_up, w_down, w_pg, w_pe))
    gains = tuple(row2(g) for g in (post_mix_g, pre_ffn_g, post_ffn_g, pe_post_g))
    lamv = jnp.stack([lam_q1[i], lam_k1[i], lam_q2[i], lam_k2[i]])
    sg = row2(subln_g)
    norm_args = (row2(gm_norm_g), row2(gm_norm_b))

    xp = x_prompt.reshape(b * s, D_MODEL)
    chunk = min(s, GM_CHUNK)
    qb, kt, ktb, v_rows, vb, gvn, ogm, gates = _inproj(
        xp, row2(pre_mix_g), w_in_b, wkt_b, *norm_args, *_gmlp_blocks(gm_ws[i], gm_bs[i], chunk),
        chunk_len=chunk, seq_len=s, tm=256)
    oa = _prompt_attn(lamv, qb.reshape(b, s, -1), ktb, vb.reshape(b, s, -1), sg, lam0=lam0, tq=256)
    yp = _tail(xp, oa.reshape(b * s, -1), ogm, gates, p_prompt[i].reshape(b * s, -1), wts, gains, tm=256)
    k_prompt = jnp.transpose(kt.reshape(1, b, N_HEADS, 2, QK_DIM, s), (0, 1, 5, 2, 3, 4))
    prompt_out = (yp.reshape(b, s, D_MODEL), k_prompt, v_rows.reshape(1, b, s, N_HEADS, HEAD_DIM),
                  gvn.reshape(1, b, s, GM_WIDTH))

    xs = x_sample.reshape(n * t_new, D_MODEL)
    chunk = min(t_new, GM_CHUNK)
    q, k, v, gvn, ogm, gates = _inproj(
        xs, row2(pre_mix_g), w_in_b, None, *norm_args, *_gmlp_blocks(gm_ws[i], gm_bs[i], chunk),
        chunk_len=chunk, seq_len=t_new, tm=256)
    cache_kt = jnp.transpose(cache_k, (0, 1, 3, 4, 5, 2)).reshape(depth * n_pool, QK_WIDTH, PAGE)
    cache_vr = cache_v.reshape(depth * n_pool, PAGE * N_HEADS, HEAD_DIM)
    oa = _sample_attn(page_table, lamv, q.reshape(n, t_new, -1), k.reshape(n, t_new, -1),
                      v.reshape(n, t_new, -1), sg, cache_kt, cache_vr, page0=i * n_pool, lam0=lam0)
    ys = _tail(xs, oa.reshape(n * t_new, -1), ogm, gates, p_sample[i].reshape(n * t_new, -1), wts, gains, tm=256)
    sample_out = (ys.reshape(n, t_new, D_MODEL),
                  k.reshape(1, n, t_new, N_HEADS, 2, QK_DIM), v.reshape(1, n, t_new, N_HEADS, HEAD_DIM),
                  gvn.reshape(1, n, t_new, GM_WIDTH))

    return (prompt_out[0], sample_out[0]) + prompt_out[1:] + sample_out[1:]
```

```python
import functools
import math

import jax
import jax.numpy as jnp
from jax import lax
from jax.experimental import pallas as pl
from jax.experimental.pallas import tpu as pltpu

F32 = jnp.float32
BF16 = jnp.bfloat16

D_MODEL = 1024
N_HEADS = 8
QK_DIM = 64
HEAD_DIM = 2 * QK_DIM
QK_WIDTH = N_HEADS * HEAD_DIM
V_WIDTH = N_HEADS * HEAD_DIM
GM_GROUPS = 4
GM_CHUNK = 128
GM_GROUP_DIM = 128
GM_WIDTH = GM_GROUPS * GM_GROUP_DIM
D_FF = 4 * D_MODEL
PLE_DIM = 256
PAGE = 128
EPS = 1e-6
SUBLN_EPS = 1e-5
NEG_INF = -1e30
Q_PRESCALE = QK_DIM ** -0.5 * math.log2(math.e)

VMEM_LIMIT_BYTES = 56 * 1024 * 1024
LANES = 128

_Q0, _K0, _V0 = 0, QK_WIDTH, 2 * QK_WIDTH
_U0 = _V0 + V_WIDTH
_GV0 = _U0 + GM_WIDTH
_G0 = _GV0 + GM_WIDTH
IN_WIDTH = _G0 + 2 * D_MODEL


def _lambda_init(layer_idx):
    return 0.8 - 0.6 * math.exp(-0.3 * layer_idx)


def _rms(x, g, eps):
    return x * lax.rsqrt(jnp.mean(x * x, axis=-1, keepdims=True) + eps) * g


def _dot(a, b):
    return jnp.dot(a, b, preferred_element_type=F32)


def _nt_dot(a, b):
    return lax.dot_general(a, b, (((1,), (1,)), ((), ())), preferred_element_type=F32)


def _resident(shape):
    nd = len(shape)
    return pl.BlockSpec(shape, lambda *_: (0,) * nd, pipeline_mode=pl.Buffered(1))


def _params(n_grid_axes):
    return pltpu.CompilerParams(dimension_semantics=("arbitrary",) * n_grid_axes,
                                vmem_limit_bytes=VMEM_LIMIT_BYTES)


def _inproj_kernel(*refs, chunk_len, native_kv):
    if native_kv:
        (x_ref, g_ref, w_ref, wkt_ref, gmg_ref, gmb_ref, ws_ref, bias_ref,
         q_ref, kt_ref, ktb_ref, v_ref, vb_ref, gvn_ref, ogm_ref, gates_ref) = refs
    else:
        (x_ref, g_ref, w_ref, gmg_ref, gmb_ref, ws_ref, bias_ref,
         q_ref, k_ref, v_ref, gvn_ref, ogm_ref, gates_ref) = refs
    tm = x_ref.shape[0]
    h = _rms(x_ref[...], g_ref[...], EPS).astype(BF16)

    def seg(a, b):
        return _dot(h, w_ref[:, a:b])

    q_ref[...] = (seg(_Q0, _K0) * Q_PRESCALE).astype(q_ref.dtype)
    zv = seg(_V0, _U0)
    if native_kv:
        zkt = _nt_dot(wkt_ref[...], h)
        kt_ref[0] = zkt
        ktb_ref[0] = zkt.astype(BF16)
        for hd in range(N_HEADS):
            v_ref[pl.ds(hd, tm, stride=N_HEADS), :] = zv[:, hd * HEAD_DIM:(hd + 1) * HEAD_DIM]
        vb_ref[...] = zv.astype(BF16)
    else:
        k_ref[...] = seg(_K0, _V0)
        v_ref[...] = zv
    gates_ref[...] = jax.nn.sigmoid(seg(_G0, IN_WIDTH))

    u = jax.nn.gelu(seg(_U0, _GV0))
    gv = jax.nn.gelu(seg(_GV0, _G0))
    mu = jnp.mean(gv, axis=-1, keepdims=True)
    xc = gv - mu
    vn = xc * lax.rsqrt(jnp.mean(xc * xc, axis=-1, keepdims=True) + EPS) * gmg_ref[...] + gmb_ref[...]
    gvn_ref[...] = vn
    vnb = vn.astype(BF16)

    row = lax.broadcasted_iota(jnp.int32, (GM_CHUNK, GM_CHUNK), 0)
    col = lax.broadcasted_iota(jnp.int32, (GM_CHUNK, GM_CHUNK), 1)
    sh = chunk_len.bit_length() - 1
    keep = (col <= row) & ((row >> sh) == (col >> sh))
    for g in range(GM_GROUPS):
        wm = jnp.where(keep, ws_ref[g], 0.0).astype(BF16)
        cs = slice(g * GM_GROUP_DIM, (g + 1) * GM_GROUP_DIM)
        for c in range(tm // GM_CHUNK):
            rs = slice(c * GM_CHUNK, (c + 1) * GM_CHUNK)
            mixed = _dot(wm, vnb[rs, cs]) + bias_ref[:, cs]
            ogm_ref[rs, cs] = (u[rs, cs] * mixed).astype(BF16)


def _inproj(x, pre_g, w_in_b, wkt_b, gm_g, gm_b, ws_blk, bias_blk, *, chunk_len, seq_len, tm):
    m = x.shape[0]
    native_kv = wkt_b is not None
    assert m % tm == 0 and tm % GM_CHUNK == 0
    assert chunk_len & (chunk_len - 1) == 0 and GM_CHUNK % chunk_len == 0
    row = lambda w: pl.BlockSpec((tm, w), lambda i: (i, 0))
    sds = jax.ShapeDtypeStruct
    tail_shape = [sds((m, GM_WIDTH), F32), sds((m, GM_WIDTH), BF16), sds((m, 2 * D_MODEL), F32)]
    tail_specs = [row(GM_WIDTH), row(GM_WIDTH), row(2 * D_MODEL)]
    consts = [_resident((1, GM_WIDTH)), _resident((1, GM_WIDTH)),
              _resident((GM_GROUPS, GM_CHUNK, GM_CHUNK)), _resident((GM_CHUNK, GM_WIDTH))]
    if native_kv:
        assert seq_len % tm == 0
        spb = seq_len // tm
        nb = m // seq_len
        kt_spec = pl.BlockSpec((1, QK_WIDTH, tm), lambda i: (i // spb, 0, i % spb))
        out_shape = [sds((m, QK_WIDTH), BF16), sds((nb, QK_WIDTH, seq_len), F32), sds((nb, QK_WIDTH, seq_len), BF16),
                     sds((m * N_HEADS, HEAD_DIM), F32), sds((m, V_WIDTH), BF16)] + tail_shape
        out_specs = [row(QK_WIDTH), kt_spec, kt_spec,
                     pl.BlockSpec((tm * N_HEADS, HEAD_DIM), lambda i: (i, 0)), row(V_WIDTH)] + tail_specs
        in_specs = [row(D_MODEL), _resident((1, D_MODEL)), _resident((D_MODEL, IN_WIDTH)),
                    _resident((QK_WIDTH, D_MODEL))] + consts
        args = (x, pre_g, w_in_b, wkt_b, gm_g, gm_b, ws_blk, bias_blk)
    else:
        out_shape = [sds((m, QK_WIDTH), F32), sds((m, QK_WIDTH), F32), sds((m, V_WIDTH), F32)] + tail_shape
        out_specs = [row(QK_WIDTH), row(QK_WIDTH), row(V_WIDTH)] + tail_specs
        in_specs = [row(D_MODEL), _resident((1, D_MODEL)), _resident((D_MODEL, IN_WIDTH))] + consts
        args = (x, pre_g, w_in_b, gm_g, gm_b, ws_blk, bias_blk)
    return pl.pallas_call(
        functools.partial(_inproj_kernel, chunk_len=chunk_len, native_kv=native_kv),
        grid=(m // tm,),
        in_specs=in_specs,
        out_specs=out_specs,
        out_shape=out_shape,
        compiler_params=_params(1),
        name="inproj",
    )(*args)


def _diff_lambda(lamv_ref, lam0):
    lv = lamv_ref[...]
    a = jnp.sum(lv[0:1] * lv[1:2], axis=-1, keepdims=True)
    b = jnp.sum(lv[2:3] * lv[3:4], axis=-1, keepdims=True)
    return jnp.exp(a) - jnp.exp(b) + lam0


def _subln(o, sg, lam0):
    return _rms(o, sg, SUBLN_EPS) * (1.0 - lam0)


def _prompt_attn_kernel(lamv_ref, q_ref, kt_ref, v_ref, sg_ref, o_ref, *, tq, nq, heads, lam0):
    qi = pl.program_id(2)
    lam = _diff_lambda(lamv_ref, lam0)
    lane = lax.broadcasted_iota(jnp.int32, (tq, HEAD_DIM), 1)
    r = lax.broadcasted_iota(jnp.int32, (tq, tq), 0)
    c = lax.broadcasted_iota(jnp.int32, (tq, tq), 1)
    causal = c <= r

    def softmax_pv(qz, hs, kv):
        s = _dot(qz, kt_ref[0, hs, :kv])
        diag = jnp.where(causal, s[:, kv - tq:], NEG_INF)
        s = diag if kv == tq else jnp.concatenate([s[:, :kv - tq], diag], axis=1)
        e = jnp.exp2(s - jnp.max(s, axis=-1, keepdims=True))
        acc = _dot(e.astype(BF16), v_ref[0, :kv, hs])
        return acc / jnp.sum(e, axis=-1, keepdims=True)

    for ci in range(nq):
        @pl.when(qi == ci)
        def _(ci=ci):
            kv = (ci + 1) * tq
            for hd in range(heads):
                hs = slice(hd * HEAD_DIM, (hd + 1) * HEAD_DIM)
                q = q_ref[0, :, hs]
                zero = jnp.zeros_like(q)
                o = (softmax_pv(jnp.where(lane < QK_DIM, q, zero), hs, kv)
                     - lam * softmax_pv(jnp.where(lane >= QK_DIM, q, zero), hs, kv))
                o_ref[0, :, hs] = _subln(o, sg_ref[...], lam0).astype(o_ref.dtype)


def _prompt_attn(lamv, qb, ktb, vb, subln_g, *, lam0, tq, heads):
    b, s, _ = qb.shape
    nq = s // tq
    assert s % tq == 0 and N_HEADS % heads == 0
    w = heads * HEAD_DIM
    return pl.pallas_call(
        functools.partial(_prompt_attn_kernel, tq=tq, nq=nq, heads=heads, lam0=lam0),
        grid=(b, N_HEADS // heads, nq),
        in_specs=[pl.BlockSpec((4, QK_DIM), lambda bi, h, qi: (0, 0)),
                  pl.BlockSpec((1, tq, w), lambda bi, h, qi: (bi, qi, h)),
                  pl.BlockSpec((1, w, s), lambda bi, h, qi: (bi, h, 0)),
                  pl.BlockSpec((1, s, w), lambda bi, h, qi: (bi, 0, h)),
                  pl.BlockSpec((1, HEAD_DIM), lambda bi, h, qi: (0, 0))],
        out_specs=pl.BlockSpec((1, tq, w), lambda bi, h, qi: (bi, qi, h)),
        out_shape=jax.ShapeDtypeStruct((b, s, V_WIDTH), BF16),
        compiler_params=_params(3),
        name="prompt_attn",
    )(lamv, qb, ktb, vb, subln_g)


def _ffn_slice(h, wup, wdn):
    up = jnp.maximum(_dot(h, wup), 0.0)
    return _dot((up * up).astype(BF16), wdn)


def _ffn_kernel(h_ref, wup_ref, wdn_ref, f_ref):
    f_ref[...] = _ffn_slice(h_ref[...], wup_ref[...], wdn_ref[...])


def _ffn(h, w_up_b, w_down_b, *, tm):
    m = h.shape[0]
    assert m % tm == 0
    row = pl.BlockSpec((tm, D_MODEL), lambda i: (i, 0))
    return pl.pallas_call(
        _ffn_kernel,
        grid=(m // tm,),
        in_specs=[row, _resident(w_up_b.shape), _resident(w_down_b.shape)],
        out_specs=row,
        out_shape=jax.ShapeDtypeStruct((m, D_MODEL), F32),
        compiler_params=_params(1),
        name="ffn",
    )(h, w_up_b, w_down_b)


PAGES_PER_STEP = 8
EXP_CHUNK = 1024


def _sample_attn_kernel(pt_ref, lamv_ref, q_ref, kn_ref, vn_ref, sg_ref, h_ref, wup_ref, wdn_ref, *refs,
                        n_seq, n_pages, t_new, n_ffn_tiles, lam0):
    del pt_ref
    P = PAGES_PER_STEP
    k_refs, v_refs = refs[:P], refs[P:2 * P]
    o_ref, f_ref, s_scr, e_scr, m_scr, l_scr, acc_scr = refs[2 * P:]
    r = pl.program_id(0)
    t = pl.program_id(1)
    k_steps = n_pages // P
    past = n_pages * PAGE
    rows = N_HEADS * 2 * t_new
    tsh = t_new.bit_length() - 1
    qsh = QK_DIM.bit_length() - 1
    last = t == k_steps - 1
    has_prev = r >= 1
    has_cur = r < n_seq

    @pl.when(r < n_ffn_tiles)
    def _():
        width = D_FF // k_steps
        c0 = pl.multiple_of(t * width, width)
        part = _ffn_slice(h_ref[...], wup_ref[:, pl.ds(c0, width)], wdn_ref[pl.ds(c0, width), :])

        @pl.when(t == 0)
        def _():
            f_ref[...] = part

        @pl.when(t > 0)
        def _():
            f_ref[...] += part

    def lane_blocks(x):
        return [x[:, c * LANES:(c + 1) * LANES] for c in range(x.shape[1] // LANES)]

    def pv(e_cols, head_rows):
        outs = []
        for cb in range(N_HEADS // 2):
            e2 = e_scr[cb * 4 * t_new:(cb + 1) * 4 * t_new, e_cols]
            v2 = jnp.concatenate([head_rows(2 * cb), head_rows(2 * cb + 1)], axis=1).astype(BF16)
            r2 = _dot(e2, v2)
            outs += [r2[:2 * t_new, :HEAD_DIM], r2[2 * t_new:, HEAD_DIM:]]
        return jnp.concatenate(outs, axis=1)

    @pl.when(has_prev)
    def _():
        @pl.when(t == 0)
        def _():
            acc_scr[...] = jnp.zeros(acc_scr.shape, F32)

        acc = acc_scr[...]
        for i in range(P):
            off = pl.multiple_of((t * P + i) * PAGE, PAGE)
            v_ref = v_refs[i]
            acc = acc + pv(pl.ds(off, PAGE), lambda hd: v_ref[0, pl.ds(hd, PAGE, stride=N_HEADS), :])
        acc_scr[...] = acc

    @pl.when(has_prev & last)
    def _():
        lam = _diff_lambda(lamv_ref, lam0)
        vn = jnp.concatenate([vn_ref[0], jnp.zeros((PAGE - t_new, V_WIDTH), F32)], axis=0)
        acc = acc_scr[...] + pv(slice(past, past + PAGE), lambda hd: vn[:, hd * HEAD_DIM:(hd + 1) * HEAD_DIM])
        inv = 1.0 / l_scr[...]
        sg = sg_ref[...]
        outs = []
        for h in range(N_HEADS):
            a = acc[:, h * HEAD_DIM:(h + 1) * HEAD_DIM]
            i1 = inv[h * 2 * t_new:h * 2 * t_new + t_new]
            i2 = inv[h * 2 * t_new + t_new:(h + 1) * 2 * t_new]
            outs.append(_subln(a[:t_new] * i1 - lam * (a[t_new:] * i2), sg, lam0))
        o_ref[0] = jnp.concatenate(outs, axis=1)

    def q_blockdiag():
        q = q_ref[0]
        qt = jnp.concatenate([q] * (rows // t_new), axis=0)
        rr = lax.broadcasted_iota(jnp.int32, qt.shape, 0)
        cc = lax.broadcasted_iota(jnp.int32, qt.shape, 1)
        return jnp.where((cc >> qsh) == (rr >> tsh), qt, 0.0).astype(BF16)

    @pl.when(has_cur)
    def _():
        @pl.when(t == 0)
        def _():
            m_scr[...] = jnp.full(m_scr.shape, NEG_INF, F32)

        qbd = q_blockdiag()
        m = m_scr[...]
        for i in range(0, P, 2):
            kt2 = jnp.concatenate([k_refs[i][0], k_refs[i + 1][0]], axis=1).astype(BF16)
            s = _dot(qbd, kt2)
            off = pl.multiple_of((t * P + i) * PAGE, 2 * PAGE)
            s_scr[:, pl.ds(off, 2 * PAGE)] = s
            for blk in lane_blocks(s):
                m = jnp.maximum(m, blk)
        m_scr[...] = m

    @pl.when(has_cur & last)
    def _():
        qbd = q_blockdiag()
        kn = jnp.concatenate([kn_ref[0], jnp.zeros((PAGE - t_new, QK_WIDTH), F32)], axis=0).astype(BF16)
        s = _nt_dot(qbd, kn)
        rr = lax.broadcasted_iota(jnp.int32, s.shape, 0)
        cc = lax.broadcasted_iota(jnp.int32, s.shape, 1)
        s = jnp.where(cc <= (rr & (t_new - 1)), s, NEG_INF)
        s_scr[:, past:past + PAGE] = s
        m = jnp.max(jnp.maximum(m_scr[...], s), axis=-1, keepdims=True)
        mb = jnp.broadcast_to(m, (rows, LANES))

        def exp_cols(cols, width, l):
            e = jnp.exp2(s_scr[:, cols] - jnp.concatenate([mb] * (width // LANES), axis=1))
            e_scr[:, cols] = e.astype(BF16)
            for blk in lane_blocks(e):
                l = l + blk
            return l

        l = lax.fori_loop(
            0, past // EXP_CHUNK,
            lambda j, l: exp_cols(pl.ds(pl.multiple_of(j * EXP_CHUNK, EXP_CHUNK), EXP_CHUNK), EXP_CHUNK, l),
            jnp.zeros((rows, LANES), F32))
        l = exp_cols(slice(past, past + PAGE), PAGE, l)
        l_scr[...] = jnp.sum(l, axis=-1, keepdims=True)


def _sample_attn_ffn(page_table, lamv, q, k_new, v_new, subln_g, cache_kt, cache_vr, h_ffn, w_up_b, w_down_b,
                     *, page0, lam0, tm):
    n, t_new, _ = q.shape
    n_pages = page_table.shape[1]
    P = PAGES_PER_STEP
    rows = N_HEADS * 2 * t_new
    past = n_pages * PAGE
    assert n_pages % P == 0 and P % 2 == 0 and past % EXP_CHUNK == 0
    assert t_new & (t_new - 1) == 0 and t_new % 8 == 0
    k_steps = n_pages // P
    m_ffn = h_ffn.shape[0]
    n_ffn_tiles = m_ffn // tm
    assert m_ffn % tm == 0 and n_ffn_tiles <= n + 1 and D_FF % k_steps == 0 and (D_FF // k_steps) % LANES == 0

    def cur(ri):
        return jnp.minimum(ri, n - 1)

    def prev(ri):
        return jnp.maximum(ri - 1, 0)

    def k_spec(i):
        def index(ri, t, pt):
            step = jnp.where(ri < n, t, k_steps - 1)
            return (page0 + pt[cur(ri), step * P + i], 0, 0)
        return pl.BlockSpec((1, QK_WIDTH, PAGE), index)

    def v_spec(i):
        def index(ri, t, pt):
            step = jnp.where(ri >= 1, t, 0)
            return (page0 + pt[prev(ri), step * P + i], 0, 0)
        return pl.BlockSpec((1, PAGE * N_HEADS, HEAD_DIM), index)

    tok_cur = pl.BlockSpec((1, t_new, QK_WIDTH), lambda ri, t, pt: (cur(ri), 0, 0))
    tok_prev = pl.BlockSpec((1, t_new, V_WIDTH), lambda ri, t, pt: (prev(ri), 0, 0))
    ffn_row = pl.BlockSpec((tm, D_MODEL), lambda ri, t, pt: (jnp.minimum(ri, n_ffn_tiles - 1), 0))
    grid_spec = pltpu.PrefetchScalarGridSpec(
        num_scalar_prefetch=1,
        grid=(n + 1, k_steps),
        in_specs=[pl.BlockSpec((4, QK_DIM), lambda ri, t, pt: (0, 0)), tok_cur, tok_cur, tok_prev,
                  pl.BlockSpec((1, HEAD_DIM), lambda ri, t, pt: (0, 0)),
                  ffn_row, _resident(w_up_b.shape), _resident(w_down_b.shape)]
                 + [k_spec(i) for i in range(P)] + [v_spec(i) for i in range(P)],
        out_specs=[tok_prev, ffn_row],
        scratch_shapes=[pltpu.VMEM((rows, past + PAGE), F32),
                        pltpu.VMEM((rows, past + PAGE), BF16),
                        pltpu.VMEM((rows, LANES), F32),
                        pltpu.VMEM((rows, 1), F32),
                        pltpu.VMEM((2 * t_new, V_WIDTH), F32)],
    )
    return pl.pallas_call(
        functools.partial(_sample_attn_kernel, n_seq=n, n_pages=n_pages, t_new=t_new,
                          n_ffn_tiles=n_ffn_tiles, lam0=lam0),
        grid_spec=grid_spec,
        out_shape=[jax.ShapeDtypeStruct((n, t_new, V_WIDTH), F32), jax.ShapeDtypeStruct((m_ffn, D_MODEL), F32)],
        compiler_params=_params(2),
        name="sample_attn_ffn",
    )(page_table, lamv, q, k_new, v_new, subln_g, h_ffn, w_up_b, w_down_b, *([cache_kt] * P), *([cache_vr] * P))


def _mix_kernel(x_ref, oa_ref, ogm_ref, gates_ref, wba_ref, wbg_ref, wo_ref, g_mix_ref, g_pre_ref, x1_ref, h_ref):
    gates = gates_ref[...]
    merged = (gates[:, :D_MODEL] * _dot(oa_ref[...].astype(BF16), wba_ref[...])
              + gates[:, D_MODEL:] * _dot(ogm_ref[...], wbg_ref[...]))
    x = x_ref[...] + _rms(_dot(merged.astype(BF16), wo_ref[...]), g_mix_ref[...], EPS)
    x1_ref[...] = x
    h_ref[...] = _rms(x, g_pre_ref[...], EPS).astype(BF16)


def _mix(x, oa, ogm, gates, wts, gains, *, tm):
    m = x.shape[0]
    assert m % tm == 0
    row = lambda w: pl.BlockSpec((tm, w), lambda i: (i, 0))
    return pl.pallas_call(
        _mix_kernel,
        grid=(m // tm,),
        in_specs=[row(D_MODEL), row(V_WIDTH), row(GM_WIDTH), row(2 * D_MODEL)]
                 + [_resident(w.shape) for w in wts] + [_resident(g.shape) for g in gains],
        out_specs=[row(D_MODEL), row(D_MODEL)],
        out_shape=[jax.ShapeDtypeStruct((m, D_MODEL), F32), jax.ShapeDtypeStruct((m, D_MODEL), BF16)],
        compiler_params=_params(1),
        name="mix",
    )(x, oa, ogm, gates, *wts, *gains)


def _out_kernel(x1_ref, f_ref, pe_ref, wpg_ref, wpe_ref, g_ffn_ref, g_pe_ref, y_ref):
    x = x1_ref[...] + _rms(f_ref[...], g_ffn_ref[...], EPS)
    gate = jax.nn.sigmoid(_dot(x.astype(BF16), wpg_ref[...]))
    y_ref[...] = x + _rms(gate * _dot(pe_ref[...].astype(BF16), wpe_ref[...]), g_pe_ref[...], EPS)


def _out(x1, f, pe, wts, gains, *, tm):
    m = x1.shape[0]
    assert m % tm == 0
    row = lambda w: pl.BlockSpec((tm, w), lambda i: (i, 0))
    return pl.pallas_call(
        _out_kernel,
        grid=(m // tm,),
        in_specs=[row(D_MODEL), row(D_MODEL), row(PLE_DIM)]
                 + [_resident(w.shape) for w in wts] + [_resident(g.shape) for g in gains],
        out_specs=row(D_MODEL),
        out_shape=jax.ShapeDtypeStruct((m, D_MODEL), F32),
        compiler_params=_params(1),
        name="out",
    )(x1, f, pe, *wts, *gains)


def _gmlp_blocks(ws, bs, chunk_len):
    rep = GM_CHUNK // chunk_len
    ws_blk = jnp.tile(ws[:, :chunk_len, :chunk_len], (1, rep, rep))
    bias = jnp.tile(jnp.transpose(bs[:, :chunk_len]), (rep, 1))
    return ws_blk, jnp.repeat(bias, GM_GROUP_DIM, axis=1)


def kernel(x_prompt, x_sample, cache_k, cache_v, page_table, p_prompt, p_sample, pre_mix_g, w_in, lam_q1, lam_k1, lam_q2, lam_k2, subln_g, gm_norm_g, gm_norm_b, gm_ws, gm_bs, w_branch_attn, w_branch_gm, w_o, post_mix_g, pre_ffn_g, w_up, w_down, post_ffn_g, w_pe, w_pg, pe_post_g):
    depth, n_pool = cache_k.shape[0], cache_k.shape[1]
    assert depth == 1
    b, s, _ = x_prompt.shape
    n, t_new, _ = x_sample.shape
    i = 0
    tm = 256
    lam0 = _lambda_init(i)
    row2 = lambda a: a[i].reshape(1, -1)
    bf = lambda w: w[i].astype(BF16)
    w_in_b = bf(w_in)
    wkt_b = jnp.transpose(w_in[i][:, _K0:_V0]).astype(BF16)
    mix_w = (bf(w_branch_attn), bf(w_branch_gm), bf(w_o))
    mix_g = (row2(post_mix_g), row2(pre_ffn_g))
    w_up_b, w_down_b = bf(w_up), bf(w_down)
    out_w = (bf(w_pg), bf(w_pe))
    out_g = (row2(post_ffn_g), row2(pe_post_g))
    lamv = jnp.stack([lam_q1[i], lam_k1[i], lam_q2[i], lam_k2[i]])
    sg = row2(subln_g)
    norm_args = (row2(gm_norm_g), row2(gm_norm_b))

    xp = x_prompt.reshape(b * s, D_MODEL)
    chunk = min(s, GM_CHUNK)
    qb, kt, ktb, v_rows, vb, gvn_p, ogm, gates = _inproj(
        xp, row2(pre_mix_g), w_in_b, wkt_b, *norm_args, *_gmlp_blocks(gm_ws[i], gm_bs[i], chunk),
        chunk_len=chunk, seq_len=s, tm=tm)
    oa = _prompt_attn(lamv, qb.reshape(b, s, -1), ktb, vb.reshape(b, s, -1), sg, lam0=lam0, tq=256, heads=2)
    x1_p, h_p = _mix(xp, oa.reshape(b * s, -1), ogm, gates, mix_w, mix_g, tm=tm)

    xs = x_sample.reshape(n * t_new, D_MODEL)
    chunk = min(t_new, GM_CHUNK)
    q, k_s, v_s, gvn_s, ogm, gates = _inproj(
        xs, row2(pre_mix_g), w_in_b, None, *norm_args, *_gmlp_blocks(gm_ws[i], gm_bs[i], chunk),
        chunk_len=chunk, seq_len=t_new, tm=tm)
    cache_kt = jnp.transpose(cache_k, (0, 1, 3, 4, 5, 2)).reshape(depth * n_pool, QK_WIDTH, PAGE)
    cache_vr = cache_v.reshape(depth * n_pool, PAGE * N_HEADS, HEAD_DIM)
    oa, f_p = _sample_attn_ffn(page_table, lamv, q.reshape(n, t_new, -1), k_s.reshape(n, t_new, -1),
                               v_s.reshape(n, t_new, -1), sg, cache_kt, cache_vr, h_p, w_up_b, w_down_b,
                               page0=i * n_pool, lam0=lam0, tm=tm)

    yp = _out(x1_p, f_p, p_prompt[i].reshape(b * s, -1), out_w, out_g, tm=tm)
    x1_s, h_s = _mix(xs, oa.reshape(n * t_new, -1), ogm, gates, mix_w, mix_g, tm=tm)
    ys = _out(x1_s, _ffn(h_s, w_up_b, w_down_b, tm=tm), p_sample[i].reshape(n * t_new, -1), out_w, out_g, tm=tm)

    k_prompt = jnp.transpose(kt.reshape(1, b, N_HEADS, 2, QK_DIM, s), (0, 1, 5, 2, 3, 4))
    return (yp.reshape(b, s, D_MODEL), ys.reshape(n, t_new, D_MODEL),
            k_prompt, v_rows.reshape(1, b, s, N_HEADS, HEAD_DIM), gvn_p.reshape(1, b, s, GM_WIDTH),
            k_s.reshape(1, n, t_new, N_HEADS, 2, QK_DIM), v_s.reshape(1, n, t_new, N_HEADS, HEAD_DIM),
            gvn_s.reshape(1, n, t_new, GM_WIDTH))
```

```python
import functools
import math

import jax
import jax.numpy as jnp
from jax import lax
from jax.experimental import pallas as pl
from jax.experimental.pallas import tpu as pltpu

F32 = jnp.float32
BF16 = jnp.bfloat16

D_MODEL = 1024
N_HEADS = 8
QK_DIM = 64
HEAD_DIM = 2 * QK_DIM
QK_WIDTH = N_HEADS * HEAD_DIM
V_WIDTH = N_HEADS * HEAD_DIM
GM_GROUPS = 4
GM_CHUNK = 128
GM_GROUP_DIM = 128
GM_WIDTH = GM_GROUPS * GM_GROUP_DIM
D_FF = 4 * D_MODEL
PLE_DIM = 256
PAGE = 128
EPS = 1e-6
SUBLN_EPS = 1e-5
NEG_INF = -1e30
Q_PRESCALE = QK_DIM ** -0.5 * math.log2(math.e)

VMEM_LIMIT_BYTES = 56 * 1024 * 1024
LANES = 128

_Q0, _K0, _V0 = 0, QK_WIDTH, 2 * QK_WIDTH
_U0 = _V0 + V_WIDTH
_GV0 = _U0 + GM_WIDTH
_G0 = _GV0 + GM_WIDTH
IN_WIDTH = _G0 + 2 * D_MODEL


def _lambda_init(layer_idx):
    return 0.8 - 0.6 * math.exp(-0.3 * layer_idx)


def _rms(x, g, eps):
    return x * lax.rsqrt(jnp.mean(x * x, axis=-1, keepdims=True) + eps) * g


def _dot(a, b):
    return jnp.dot(a, b, preferred_element_type=F32)


def _nt_dot(a, b):
    return lax.dot_general(a, b, (((1,), (1,)), ((), ())), preferred_element_type=F32)


def _resident(shape):
    nd = len(shape)
    return pl.BlockSpec(shape, lambda *_: (0,) * nd, pipeline_mode=pl.Buffered(1))


def _params(n_grid_axes):
    return pltpu.CompilerParams(dimension_semantics=("arbitrary",) * n_grid_axes,
                                vmem_limit_bytes=VMEM_LIMIT_BYTES)


def _inproj_kernel(*refs, chunk_len, native_kv):
    if native_kv:
        (x_ref, g_ref, w_ref, wkt_ref, gmg_ref, gmb_ref, ws_ref, bias_ref,
         q_ref, kt_ref, ktb_ref, v_ref, vb_ref, gvn_ref, ogm_ref, gates_ref) = refs
    else:
        (x_ref, g_ref, w_ref, gmg_ref, gmb_ref, ws_ref, bias_ref,
         q_ref, k_ref, v_ref, gvn_ref, ogm_ref, gates_ref) = refs
    tm = x_ref.shape[0]
    h = _rms(x_ref[...], g_ref[...], EPS).astype(BF16)

    def seg(a, b):
        return _dot(h, w_ref[:, a:b])

    q_ref[...] = (seg(_Q0, _K0) * Q_PRESCALE).astype(q_ref.dtype)
    zv = seg(_V0, _U0)
    if native_kv:
        zkt = _nt_dot(wkt_ref[...], h)
        kt_ref[0] = zkt
        ktb_ref[0] = zkt.astype(BF16)
        for hd in range(N_HEADS):
            v_ref[pl.ds(hd, tm, stride=N_HEADS), :] = zv[:, hd * HEAD_DIM:(hd + 1) * HEAD_DIM]
        vb_ref[...] = zv.astype(BF16)
    else:
        k_ref[...] = seg(_K0, _V0)
        v_ref[...] = zv
    gates_ref[...] = jax.nn.sigmoid(seg(_G0, IN_WIDTH))

    u = jax.nn.gelu(seg(_U0, _GV0))
    gv = jax.nn.gelu(seg(_GV0, _G0))
    mu = jnp.mean(gv, axis=-1, keepdims=True)
    xc = gv - mu
    vn = xc * lax.rsqrt(jnp.mean(xc * xc, axis=-1, keepdims=True) + EPS) * gmg_ref[...] + gmb_ref[...]
    gvn_ref[...] = vn
    vnb = vn.astype(BF16)

    row = lax.broadcasted_iota(jnp.int32, (GM_CHUNK, GM_CHUNK), 0)
    col = lax.broadcasted_iota(jnp.int32, (GM_CHUNK, GM_CHUNK), 1)
    sh = chunk_len.bit_length() - 1
    keep = (col <= row) & ((row >> sh) == (col >> sh))
    for g in range(GM_GROUPS):
        wm = jnp.where(keep, ws_ref[g], 0.0).astype(BF16)
        cs = slice(g * GM_GROUP_DIM, (g + 1) * GM_GROUP_DIM)
        for c in range(tm // GM_CHUNK):
            rs = slice(c * GM_CHUNK, (c + 1) * GM_CHUNK)
            mixed = _dot(wm, vnb[rs, cs]) + bias_ref[:, cs]
            ogm_ref[rs, cs] = (u[rs, cs] * mixed).astype(BF16)


def _inproj(x, pre_g, w_in_b, wkt_b, gm_g, gm_b, ws_blk, bias_blk, *, chunk_len, seq_len, tm):
    m = x.shape[0]
    native_kv = wkt_b is not None
    assert m % tm == 0 and tm % GM_CHUNK == 0
    assert chunk_len & (chunk_len - 1) == 0 and GM_CHUNK % chunk_len == 0
    row = lambda w: pl.BlockSpec((tm, w), lambda i: (i, 0))
    sds = jax.ShapeDtypeStruct
    tail_shape = [sds((m, GM_WIDTH), F32), sds((m, GM_WIDTH), BF16), sds((m, 2 * D_MODEL), F32)]
    tail_specs = [row(GM_WIDTH), row(GM_WIDTH), row(2 * D_MODEL)]
    consts = [_resident((1, GM_WIDTH)), _resident((1, GM_WIDTH)),
              _resident((GM_GROUPS, GM_CHUNK, GM_CHUNK)), _resident((GM_CHUNK, GM_WIDTH))]
    if native_kv:
        assert seq_len % tm == 0
        spb = seq_len // tm
        nb = m // seq_len
        kt_spec = pl.BlockSpec((1, QK_WIDTH, tm), lambda i: (i // spb, 0, i % spb))
        out_shape = [sds((m, QK_WIDTH), BF16), sds((nb, QK_WIDTH, seq_len), F32), sds((nb, QK_WIDTH, seq_len), BF16),
                     sds((m * N_HEADS, HEAD_DIM), F32), sds((m, V_WIDTH), BF16)] + tail_shape
        out_specs = [row(QK_WIDTH), kt_spec, kt_spec,
                     pl.BlockSpec((tm * N_HEADS, HEAD_DIM), lambda i: (i, 0)), row(V_WIDTH)] + tail_specs
        in_specs = [row(D_MODEL), _resident((1, D_MODEL)), _resident((D_MODEL, IN_WIDTH)),
                    _resident((QK_WIDTH, D_MODEL))] + consts
        args = (x, pre_g, w_in_b, wkt_b, gm_g, gm_b, ws_blk, bias_blk)
    else:
        out_shape = [sds((m, QK_WIDTH), F32), sds((m, QK_WIDTH), F32), sds((m, V_WIDTH), F32)] + tail_shape
        out_specs = [row(QK_WIDTH), row(QK_WIDTH), row(V_WIDTH)] + tail_specs
        in_specs = [row(D_MODEL), _resident((1, D_MODEL)), _resident((D_MODEL, IN_WIDTH))] + consts
        args = (x, pre_g, w_in_b, gm_g, gm_b, ws_blk, bias_blk)
    return pl.pallas_call(
        functools.partial(_inproj_kernel, chunk_len=chunk_len, native_kv=native_kv),
        grid=(m // tm,),
        in_specs=in_specs,
        out_specs=out_specs,
        out_shape=out_shape,
        compiler_params=_params(1),
        name="inproj",
    )(*args)


def _diff_lambda(lamv_ref, lam0):
    lv = lamv_ref[...]
    a = jnp.sum(lv[0:1] * lv[1:2], axis=-1, keepdims=True)
    b = jnp.sum(lv[2:3] * lv[3:4], axis=-1, keepdims=True)
    return jnp.exp(a) - jnp.exp(b) + lam0


def _subln(o, sg, lam0):
    return _rms(o, sg, SUBLN_EPS) * (1.0 - lam0)


def _prompt_attn_kernel(lamv_ref, q_ref, kt_ref, v_ref, sg_ref, o_ref, *, tq, nq, heads, lam0):
    qi = pl.program_id(2)
    lam = _diff_lambda(lamv_ref, lam0)
    lane = lax.broadcasted_iota(jnp.int32, (tq, HEAD_DIM), 1)
    r = lax.broadcasted_iota(jnp.int32, (tq, tq), 0)
    c = lax.broadcasted_iota(jnp.int32, (tq, tq), 1)
    causal = c <= r

    def softmax_pv(qz, hs, kv):
        s = _dot(qz, kt_ref[0, hs, :kv])
        diag = jnp.where(causal, s[:, kv - tq:], NEG_INF)
        s = diag if kv == tq else jnp.concatenate([s[:, :kv - tq], diag], axis=1)
        e = jnp.exp2(s - jnp.max(s, axis=-1, keepdims=True))
        acc = _dot(e.astype(BF16), v_ref[0, :kv, hs])
        return acc / jnp.sum(e, axis=-1, keepdims=True)

    for ci in range(nq):
        @pl.when(qi == ci)
        def _(ci=ci):
            kv = (ci + 1) * tq
            for hd in range(heads):
                hs = slice(hd * HEAD_DIM, (hd + 1) * HEAD_DIM)
                q = q_ref[0, :, hs]
                zero = jnp.zeros_like(q)
                o = (softmax_pv(jnp.where(lane < QK_DIM, q, zero), hs, kv)
                     - lam * softmax_pv(jnp.where(lane >= QK_DIM, q, zero), hs, kv))
                o_ref[0, :, hs] = _subln(o, sg_ref[...], lam0).astype(o_ref.dtype)


def _prompt_attn(lamv, qb, ktb, vb, subln_g, *, lam0, tq, heads):
    b, s, _ = qb.shape
    nq = s // tq
    assert s % tq == 0 and N_HEADS % heads == 0
    w = heads * HEAD_DIM
    return pl.pallas_call(
        functools.partial(_prompt_attn_kernel, tq=tq, nq=nq, heads=heads, lam0=lam0),
        grid=(b, N_HEADS // heads, nq),
        in_specs=[pl.BlockSpec((4, QK_DIM), lambda bi, h, qi: (0, 0)),
                  pl.BlockSpec((1, tq, w), lambda bi, h, qi: (bi, qi, h)),
                  pl.BlockSpec((1, w, s), lambda bi, h, qi: (bi, h, 0)),
                  pl.BlockSpec((1, s, w), lambda bi, h, qi: (bi, 0, h)),
                  pl.BlockSpec((1, HEAD_DIM), lambda bi, h, qi: (0, 0))],
        out_specs=pl.BlockSpec((1, tq, w), lambda bi, h, qi: (bi, qi, h)),
        out_shape=jax.ShapeDtypeStruct((b, s, V_WIDTH), BF16),
        compiler_params=_params(3),
        name="prompt_attn",
    )(lamv, qb, ktb, vb, subln_g)


def _ffn_slice(h, wup, wdn):
    up = jnp.maximum(_dot(h, wup), 0.0)
    return _dot((up * up).astype(BF16), wdn)


def _ffn_kernel(h_ref, wup_ref, wdn_ref, f_ref):
    f_ref[...] = _ffn_slice(h_ref[...], wup_ref[...], wdn_ref[...])


def _ffn(h, w_up_b, w_down_b, *, tm):
    m = h.shape[0]
    assert m % tm == 0
    row = pl.BlockSpec((tm, D_MODEL), lambda i: (i, 0))
    return pl.pallas_call(
        _ffn_kernel,
        grid=(m // tm,),
        in_specs=[row, _resident(w_up_b.shape), _resident(w_down_b.shape)],
        out_specs=row,
        out_shape=jax.ShapeDtypeStruct((m, D_MODEL), F32),
        compiler_params=_params(1),
        name="ffn",
    )(h, w_up_b, w_down_b)


PAGES_PER_STEP = 8
PAGE_SLOTS = 3
EXP_CHUNK = 1024


def _sample_attn_kernel(pt_ref, lamv_ref, q_ref, kn_ref, vn_ref, sg_ref, h_ref, wup_ref, wdn_ref, ck_hbm, cv_hbm,
                        o_ref, f_ref, kbuf, vbuf, ksem, vsem, s_scr, e_scr, m_scr, l_scr, acc_scr,
                        *, n_seq, n_pages, t_new, n_ffn_tiles, page0, lam0):
    P = PAGES_PER_STEP
    r = pl.program_id(0)
    t = pl.program_id(1)
    k_steps = n_pages // P
    n_steps = (n_seq + 1) * k_steps
    past = n_pages * PAGE
    rows = N_HEADS * 2 * t_new
    tsh = t_new.bit_length() - 1
    qsh = QK_DIM.bit_length() - 1
    last = t == k_steps - 1
    has_prev = r >= 1
    has_cur = r < n_seq
    step = r * k_steps + t
    slot = step % PAGE_SLOTS

    def page_group(g, act):
        g_row = g // k_steps
        g_t = g - g_row * k_steps
        g_slot = g % PAGE_SLOTS

        @pl.when(g_row < n_seq)
        def _():
            for i in range(P):
                page = page0 + pt_ref[g_row, g_t * P + i]
                act(pltpu.make_async_copy(ck_hbm.at[page], kbuf.at[g_slot, i], ksem.at[g_slot, i]))

        @pl.when(g_row >= 1)
        def _():
            for i in range(P):
                page = page0 + pt_ref[g_row - 1, g_t * P + i]
                act(pltpu.make_async_copy(cv_hbm.at[page], vbuf.at[g_slot, i], vsem.at[g_slot, i]))

    @pl.when(step == 0)
    def _():
        for g in range(PAGE_SLOTS - 1):
            page_group(jnp.int32(g), lambda cp: cp.start())

    @pl.when(step + (PAGE_SLOTS - 1) < n_steps)
    def _():
        page_group(step + (PAGE_SLOTS - 1), lambda cp: cp.start())

    page_group(step, lambda cp: cp.wait())

    @pl.when(r < n_ffn_tiles)
    def _():
        width = D_FF // k_steps
        c0 = pl.multiple_of(t * width, width)
        part = _ffn_slice(h_ref[...], wup_ref[:, pl.ds(c0, width)], wdn_ref[pl.ds(c0, width), :])

        @pl.when(t == 0)
        def _():
            f_ref[...] = part

        @pl.when(t > 0)
        def _():
            f_ref[...] += part

    def lane_blocks(x):
        return [x[:, c * LANES:(c + 1) * LANES] for c in range(x.shape[1] // LANES)]

    def pv(e_cols, head_rows):
        outs = []
        for cb in range(N_HEADS // 2):
            e2 = e_scr[cb * 4 * t_new:(cb + 1) * 4 * t_new, e_cols]
            v2 = jnp.concatenate([head_rows(2 * cb), head_rows(2 * cb + 1)], axis=1).astype(BF16)
            r2 = _dot(e2, v2)
            outs += [r2[:2 * t_new, :HEAD_DIM], r2[2 * t_new:, HEAD_DIM:]]
        return jnp.concatenate(outs, axis=1)

    @pl.when(has_prev)
    def _():
        @pl.when(t == 0)
        def _():
            acc_scr[...] = jnp.zeros(acc_scr.shape, F32)

        acc = acc_scr[...]
        for i in range(P):
            off = pl.multiple_of((t * P + i) * PAGE, PAGE)
            acc = acc + pv(pl.ds(off, PAGE), lambda hd: vbuf[slot, i, pl.ds(hd, PAGE, stride=N_HEADS), :])
        acc_scr[...] = acc

    @pl.when(has_prev & last)
    def _():
        lam = _diff_lambda(lamv_ref, lam0)
        vn = jnp.concatenate([vn_ref[0], jnp.zeros((PAGE - t_new, V_WIDTH), F32)], axis=0)
        acc = acc_scr[...] + pv(slice(past, past + PAGE), lambda hd: vn[:, hd * HEAD_DIM:(hd + 1) * HEAD_DIM])
        inv = 1.0 / l_scr[...]
        sg = sg_ref[...]
        outs = []
        for h in range(N_HEADS):
            a = acc[:, h * HEAD_DIM:(h + 1) * HEAD_DIM]
            i1 = inv[h * 2 * t_new:h * 2 * t_new + t_new]
            i2 = inv[h * 2 * t_new + t_new:(h + 1) * 2 * t_new]
            outs.append(_subln(a[:t_new] * i1 - lam * (a[t_new:] * i2), sg, lam0))
        o_ref[0] = jnp.concatenate(outs, axis=1)

    def q_blockdiag():
        q = q_ref[0]
        qt = jnp.concatenate([q] * (rows // t_new), axis=0)
        rr = lax.broadcasted_iota(jnp.int32, qt.shape, 0)
        cc = lax.broadcasted_iota(jnp.int32, qt.shape, 1)
        return jnp.where((cc >> qsh) == (rr >> tsh), qt, 0.0).astype(BF16)

    @pl.when(has_cur)
    def _():
        @pl.when(t == 0)
        def _():
            m_scr[...] = jnp.full(m_scr.shape, NEG_INF, F32)

        qbd = q_blockdiag()
        m = m_scr[...]
        for i in range(0, P, 2):
            kt2 = jnp.concatenate([kbuf[slot, i], kbuf[slot, i + 1]], axis=1).astype(BF16)
            s = _dot(qbd, kt2)
            off = pl.multiple_of((t * P + i) * PAGE, 2 * PAGE)
            s_scr[:, pl.ds(off, 2 * PAGE)] = s
            for blk in lane_blocks(s):
                m = jnp.maximum(m, blk)
        m_scr[...] = m

    @pl.when(has_cur & last)
    def _():
        qbd = q_blockdiag()
        kn = jnp.concatenate([kn_ref[0], jnp.zeros((PAGE - t_new, QK_WIDTH), F32)], axis=0).astype(BF16)
        s = _nt_dot(qbd, kn)
        rr = lax.broadcasted_iota(jnp.int32, s.shape, 0)
        cc = lax.broadcasted_iota(jnp.int32, s.shape, 1)
        s = jnp.where(cc <= (rr & (t_new - 1)), s, NEG_INF)
        s_scr[:, past:past + PAGE] = s
        m = jnp.max(jnp.maximum(m_scr[...], s), axis=-1, keepdims=True)
        mb = jnp.broadcast_to(m, (rows, LANES))

        def exp_cols(cols, width, l):
            e = jnp.exp2(s_scr[:, cols] - jnp.concatenate([mb] * (width // LANES), axis=1))
            e_scr[:, cols] = e.astype(BF16)
            for blk in lane_blocks(e):
                l = l + blk
            return l

        l = lax.fori_loop(
            0, past // EXP_CHUNK,
            lambda j, l: exp_cols(pl.ds(pl.multiple_of(j * EXP_CHUNK, EXP_CHUNK), EXP_CHUNK), EXP_CHUNK, l),
            jnp.zeros((rows, LANES), F32))
        l = exp_cols(slice(past, past + PAGE), PAGE, l)
        l_scr[...] = jnp.sum(l, axis=-1, keepdims=True)


def _sample_attn_ffn(page_table, lamv, q, k_new, v_new, subln_g, cache_kt, cache_vr, h_ffn, w_up_b, w_down_b,
                     *, page0, lam0, tm):
    n, t_new, _ = q.shape
    n_pages = page_table.shape[1]
    P = PAGES_PER_STEP
    rows = N_HEADS * 2 * t_new
    past = n_pages * PAGE
    assert n_pages % P == 0 and P % 2 == 0 and past % EXP_CHUNK == 0
    assert t_new & (t_new - 1) == 0 and t_new % 8 == 0
    k_steps = n_pages // P
    m_ffn = h_ffn.shape[0]
    n_ffn_tiles = m_ffn // tm
    assert m_ffn % tm == 0 and n_ffn_tiles <= n + 1 and D_FF % k_steps == 0 and (D_FF // k_steps) % LANES == 0
    assert (n + 1) * k_steps >= PAGE_SLOTS - 1

    def cur(ri):
        return jnp.minimum(ri, n - 1)

    def prev(ri):
        return jnp.maximum(ri - 1, 0)

    tok_cur = pl.BlockSpec((1, t_new, QK_WIDTH), lambda ri, t, pt: (cur(ri), 0, 0))
    tok_prev = pl.BlockSpec((1, t_new, V_WIDTH), lambda ri, t, pt: (prev(ri), 0, 0))
    ffn_row = pl.BlockSpec((tm, D_MODEL), lambda ri, t, pt: (jnp.minimum(ri, n_ffn_tiles - 1), 0))
    hbm = pl.BlockSpec(memory_space=pl.ANY)
    grid_spec = pltpu.PrefetchScalarGridSpec(
        num_scalar_prefetch=1,
        grid=(n + 1, k_steps),
        in_specs=[pl.BlockSpec((4, QK_DIM), lambda ri, t, pt: (0, 0)), tok_cur, tok_cur, tok_prev,
                  pl.BlockSpec((1, HEAD_DIM), lambda ri, t, pt: (0, 0)),
                  ffn_row, _resident(w_up_b.shape), _resident(w_down_b.shape), hbm, hbm],
        out_specs=[tok_prev, ffn_row],
        scratch_shapes=[pltpu.VMEM((PAGE_SLOTS, P, QK_WIDTH, PAGE), F32),
                        pltpu.VMEM((PAGE_SLOTS, P, PAGE * N_HEADS, HEAD_DIM), F32),
                        pltpu.SemaphoreType.DMA((PAGE_SLOTS, P)),
                        pltpu.SemaphoreType.DMA((PAGE_SLOTS, P)),
                        pltpu.VMEM((rows, past + PAGE), F32),
                        pltpu.VMEM((rows, past + PAGE), BF16),
                        pltpu.VMEM((rows, LANES), F32),
                        pltpu.VMEM((rows, 1), F32),
                        pltpu.VMEM((2 * t_new, V_WIDTH), F32)],
    )
    return pl.pallas_call(
        functools.partial(_sample_attn_kernel, n_seq=n, n_pages=n_pages, t_new=t_new,
                          n_ffn_tiles=n_ffn_tiles, page0=page0, lam0=lam0),
        grid_spec=grid_spec,
        out_shape=[jax.ShapeDtypeStruct((n, t_new, V_WIDTH), F32), jax.ShapeDtypeStruct((m_ffn, D_MODEL), F32)],
        compiler_params=_params(2),
        name="sample_attn_ffn",
    )(page_table, lamv, q, k_new, v_new, subln_g, h_ffn, w_up_b, w_down_b, cache_kt, cache_vr)


def _mix_kernel(x_ref, oa_ref, ogm_ref, gates_ref, wba_ref, wbg_ref, wo_ref, g_mix_ref, g_pre_ref, x1_ref, h_ref):
    gates = gates_ref[...]
    merged = (gates[:, :D_MODEL] * _dot(oa_ref[...].astype(BF16), wba_ref[...])
              + gates[:, D_MODEL:] * _dot(ogm_ref[...], wbg_ref[...]))
    x = x_ref[...] + _rms(_dot(merged.astype(BF16), wo_ref[...]), g_mix_ref[...], EPS)
    x1_ref[...] = x
    h_ref[...] = _rms(x, g_pre_ref[...], EPS).astype(BF16)


def _mix(x, oa, ogm, gates, wts, gains, *, tm):
    m = x.shape[0]
    assert m % tm == 0
    row = lambda w: pl.BlockSpec((tm, w), lambda i: (i, 0))
    return pl.pallas_call(
        _mix_kernel,
        grid=(m // tm,),
        in_specs=[row(D_MODEL), row(V_WIDTH), row(GM_WIDTH), row(2 * D_MODEL)]
                 + [_resident(w.shape) for w in wts] + [_resident(g.shape) for g in gains],
        out_specs=[row(D_MODEL), row(D_MODEL)],
        out_shape=[jax.ShapeDtypeStruct((m, D_MODEL), F32), jax.ShapeDtypeStruct((m, D_MODEL), BF16)],
        compiler_params=_params(1),
        name="mix",
    )(x, oa, ogm, gates, *wts, *gains)


def _out_kernel(x1_ref, f_ref, pe_ref, wpg_ref, wpe_ref, g_ffn_ref, g_pe_ref, y_ref):
    x = x1_ref[...] + _rms(f_ref[...], g_ffn_ref[...], EPS)
    gate = jax.nn.sigmoid(_dot(x.astype(BF16), wpg_ref[...]))
    y_ref[...] = x + _rms(gate * _dot(pe_ref[...].astype(BF16), wpe_ref[...]), g_pe_ref[...], EPS)


def _out(x1, f, pe, wts, gains, *, tm):
    m = x1.shape[0]
    assert m % tm == 0
    row = lambda w: pl.BlockSpec((tm, w), lambda i: (i, 0))
    return pl.pallas_call(
        _out_kernel,
        grid=(m // tm,),
        in_specs=[row(D_MODEL), row(D_MODEL), row(PLE_DIM)]
                 + [_resident(w.shape) for w in wts] + [_resident(g.shape) for g in gains],
        out_specs=row(D_MODEL),
        out_shape=jax.ShapeDtypeStruct((m, D_MODEL), F32),
        compiler_params=_params(1),
        name="out",
    )(x1, f, pe, *wts, *gains)


def _gmlp_blocks(ws, bs, chunk_len):
    rep = GM_CHUNK // chunk_len
    ws_blk = jnp.tile(ws[:, :chunk_len, :chunk_len], (1, rep, rep))
    bias = jnp.tile(jnp.transpose(bs[:, :chunk_len]), (rep, 1))
    return ws_blk, jnp.repeat(bias, GM_GROUP_DIM, axis=1)


def kernel(x_prompt, x_sample, cache_k, cache_v, page_table, p_prompt, p_sample, pre_mix_g, w_in, lam_q1, lam_k1, lam_q2, lam_k2, subln_g, gm_norm_g, gm_norm_b, gm_ws, gm_bs, w_branch_attn, w_branch_gm, w_o, post_mix_g, pre_ffn_g, w_up, w_down, post_ffn_g, w_pe, w_pg, pe_post_g):
    depth, n_pool = cache_k.shape[0], cache_k.shape[1]
    assert depth == 1
    b, s, _ = x_prompt.shape
    n, t_new, _ = x_sample.shape
    i = 0
    tm = 256
    lam0 = _lambda_init(i)
    row2 = lambda a: a[i].reshape(1, -1)
    bf = lambda w: w[i].astype(BF16)
    w_in_b = bf(w_in)
    wkt_b = jnp.transpose(w_in[i][:, _K0:_V0]).astype(BF16)
    mix_w = (bf(w_branch_attn), bf(w_branch_gm), bf(w_o))
    mix_g = (row2(post_mix_g), row2(pre_ffn_g))
    w_up_b, w_down_b = bf(w_up), bf(w_down)
    out_w = (bf(w_pg), bf(w_pe))
    out_g = (row2(post_ffn_g), row2(pe_post_g))
    lamv = jnp.stack([lam_q1[i], lam_k1[i], lam_q2[i], lam_k2[i]])
    sg = row2(subln_g)
    norm_args = (row2(gm_norm_g), row2(gm_norm_b))

    xp = x_prompt.reshape(b * s, D_MODEL)
    chunk = min(s, GM_CHUNK)
    qb, kt, ktb, v_rows, vb, gvn_p, ogm, gates = _inproj(
        xp, row2(pre_mix_g), w_in_b, wkt_b, *norm_args, *_gmlp_blocks(gm_ws[i], gm_bs[i], chunk),
        chunk_len=chunk, seq_len=s, tm=tm)
    oa = _prompt_attn(lamv, qb.reshape(b, s, -1), ktb, vb.reshape(b, s, -1), sg, lam0=lam0, tq=256, heads=2)
    x1_p, h_p = _mix(xp, oa.reshape(b * s, -1), ogm, gates, mix_w, mix_g, tm=tm)

    xs = x_sample.reshape(n * t_new, D_MODEL)
    chunk = min(t_new, GM_CHUNK)
    q, k_s, v_s, gvn_s, ogm, gates = _inproj(
        xs, row2(pre_mix_g), w_in_b, None, *norm_args, *_gmlp_blocks(gm_ws[i], gm_bs[i], chunk),
        chunk_len=chunk, seq_len=t_new, tm=tm)
    cache_kt = jnp.transpose(cache_k, (0, 1, 3, 4, 5, 2)).reshape(depth * n_pool, QK_WIDTH, PAGE)
    cache_vr = cache_v.reshape(depth * n_pool, PAGE * N_HEADS, HEAD_DIM)
    oa, f_p = _sample_attn_ffn(page_table, lamv, q.reshape(n, t_new, -1), k_s.reshape(n, t_new, -1),
                               v_s.reshape(n, t_new, -1), sg, cache_kt, cache_vr, h_p, w_up_b, w_down_b,
                               page0=i * n_pool, lam0=lam0, tm=tm)

    yp = _out(x1_p, f_p, p_prompt[i].reshape(b * s, -1), out_w, out_g, tm=tm)
    x1_s, h_s = _mix(xs, oa.reshape(n * t_new, -1), ogm, gates, mix_w, mix_g, tm=tm)
    ys = _out(x1_s, _ffn(h_s, w_up_b, w_down_b, tm=tm), p_sample[i].reshape(n * t_new, -1), out_w, out_g, tm=tm)

    k_prompt = jnp.transpose(kt.reshape(1, b, N_HEADS, 2, QK_DIM, s), (0, 1, 5, 2, 3, 4))
    return (yp.reshape(b, s, D_MODEL), ys.reshape(n, t_new, D_MODEL),
            k_prompt, v_rows.reshape(1, b, s, N_HEADS, HEAD_DIM), gvn_p.reshape(1, b, s, GM_WIDTH),
            k_s.reshape(1, n, t_new, N_HEADS, 2, QK_DIM), v_s.reshape(1, n, t_new, N_HEADS, HEAD_DIM),
            gvn_s.reshape(1, n, t_new, GM_WIDTH))
```

```python
import functools
import math

import jax
import jax.numpy as jnp
from jax import lax
from jax.experimental import pallas as pl
from jax.experimental.pallas import tpu as pltpu

F32 = jnp.float32
BF16 = jnp.bfloat16

D_MODEL = 1024
N_HEADS = 8
QK_DIM = 64
HEAD_DIM = 2 * QK_DIM
QK_WIDTH = N_HEADS * HEAD_DIM
V_WIDTH = N_HEADS * HEAD_DIM
GM_GROUPS = 4
GM_CHUNK = 128
GM_GROUP_DIM = 128
GM_WIDTH = GM_GROUPS * GM_GROUP_DIM
D_FF = 4 * D_MODEL
PLE_DIM = 256
PAGE = 128
EPS = 1e-6
SUBLN_EPS = 1e-5
NEG_INF = -1e30
Q_PRESCALE = QK_DIM ** -0.5 * math.log2(math.e)

VMEM_LIMIT_BYTES = 56 * 1024 * 1024
LANES = 128

_Q0, _K0, _V0 = 0, QK_WIDTH, 2 * QK_WIDTH
_U0 = _V0 + V_WIDTH
_GV0 = _U0 + GM_WIDTH
_G0 = _GV0 + GM_WIDTH
IN_WIDTH = _G0 + 2 * D_MODEL


def _lambda_init(layer_idx):
    return 0.8 - 0.6 * math.exp(-0.3 * layer_idx)


def _rms(x, g, eps):
    return x * lax.rsqrt(jnp.mean(x * x, axis=-1, keepdims=True) + eps) * g


def _dot(a, b):
    return jnp.dot(a, b, preferred_element_type=F32)


def _nt_dot(a, b):
    return lax.dot_general(a, b, (((1,), (1,)), ((), ())), preferred_element_type=F32)


def _resident(shape):
    nd = len(shape)
    return pl.BlockSpec(shape, lambda *_: (0,) * nd, pipeline_mode=pl.Buffered(1))


def _params(n_grid_axes):
    return pltpu.CompilerParams(dimension_semantics=("arbitrary",) * n_grid_axes,
                                vmem_limit_bytes=VMEM_LIMIT_BYTES)


def _inproj_kernel(*refs, chunk_len, native_kv):
    if native_kv:
        (x_ref, g_ref, w_ref, wkt_ref, gmg_ref, gmb_ref, ws_ref, bias_ref,
         q_ref, kt_ref, ktb_ref, v_ref, vb_ref, gvn_ref, ogm_ref, gates_ref) = refs
    else:
        (x_ref, g_ref, w_ref, gmg_ref, gmb_ref, ws_ref, bias_ref,
         q_ref, k_ref, v_ref, gvn_ref, ogm_ref, gates_ref) = refs
    tm = x_ref.shape[0]
    h = _rms(x_ref[...], g_ref[...], EPS).astype(BF16)

    def seg(a, b):
        return _dot(h, w_ref[:, a:b])

    q_ref[...] = (seg(_Q0, _K0) * Q_PRESCALE).astype(q_ref.dtype)
    zv = seg(_V0, _U0)
    if native_kv:
        zkt = _nt_dot(wkt_ref[...], h)
        kt_ref[0] = zkt
        ktb_ref[0] = zkt.astype(BF16)
        for hd in range(N_HEADS):
            v_ref[pl.ds(hd, tm, stride=N_HEADS), :] = zv[:, hd * HEAD_DIM:(hd + 1) * HEAD_DIM]
        vb_ref[...] = zv.astype(BF16)
    else:
        k_ref[...] = seg(_K0, _V0)
        v_ref[...] = zv
    gates_ref[...] = jax.nn.sigmoid(seg(_G0, IN_WIDTH))

    u = jax.nn.gelu(seg(_U0, _GV0))
    gv = jax.nn.gelu(seg(_GV0, _G0))
    mu = jnp.mean(gv, axis=-1, keepdims=True)
    xc = gv - mu
    vn = xc * lax.rsqrt(jnp.mean(xc * xc, axis=-1, keepdims=True) + EPS) * gmg_ref[...] + gmb_ref[...]
    gvn_ref[...] = vn
    vnb = vn.astype(BF16)

    row = lax.broadcasted_iota(jnp.int32, (GM_CHUNK, GM_CHUNK), 0)
    col = lax.broadcasted_iota(jnp.int32, (GM_CHUNK, GM_CHUNK), 1)
    sh = chunk_len.bit_length() - 1
    keep = (col <= row) & ((row >> sh) == (col >> sh))
    for g in range(GM_GROUPS):
        wm = jnp.where(keep, ws_ref[g], 0.0).astype(BF16)
        cs = slice(g * GM_GROUP_DIM, (g + 1) * GM_GROUP_DIM)
        for c in range(tm // GM_CHUNK):
            rs = slice(c * GM_CHUNK, (c + 1) * GM_CHUNK)
            mixed = _dot(wm, vnb[rs, cs]) + bias_ref[:, cs]
            ogm_ref[rs, cs] = (u[rs, cs] * mixed).astype(BF16)


def _inproj(x, pre_g, w_in_b, wkt_b, gm_g, gm_b, ws_blk, bias_blk, *, chunk_len, seq_len, tm):
    m = x.shape[0]
    native_kv = wkt_b is not None
    assert m % tm == 0 and tm % GM_CHUNK == 0
    assert chunk_len & (chunk_len - 1) == 0 and GM_CHUNK % chunk_len == 0
    row = lambda w: pl.BlockSpec((tm, w), lambda i: (i, 0))
    sds = jax.ShapeDtypeStruct
    tail_shape = [sds((m, GM_WIDTH), F32), sds((m, GM_WIDTH), BF16), sds((m, 2 * D_MODEL), F32)]
    tail_specs = [row(GM_WIDTH), row(GM_WIDTH), row(2 * D_MODEL)]
    consts = [_resident((1, GM_WIDTH)), _resident((1, GM_WIDTH)),
              _resident((GM_GROUPS, GM_CHUNK, GM_CHUNK)), _resident((GM_CHUNK, GM_WIDTH))]
    if native_kv:
        assert seq_len % tm == 0
        spb = seq_len // tm
        nb = m // seq_len
        kt_spec = pl.BlockSpec((1, QK_WIDTH, tm), lambda i: (i // spb, 0, i % spb))
        out_shape = [sds((m, QK_WIDTH), BF16), sds((nb, QK_WIDTH, seq_len), F32), sds((nb, QK_WIDTH, seq_len), BF16),
                     sds((m * N_HEADS, HEAD_DIM), F32), sds((m, V_WIDTH), BF16)] + tail_shape
        out_specs = [row(QK_WIDTH), kt_spec, kt_spec,
                     pl.BlockSpec((tm * N_HEADS, HEAD_DIM), lambda i: (i, 0)), row(V_WIDTH)] + tail_specs
        in_specs = [row(D_MODEL), _resident((1, D_MODEL)), _resident((D_MODEL, IN_WIDTH)),
                    _resident((QK_WIDTH, D_MODEL))] + consts
        args = (x, pre_g, w_in_b, wkt_b, gm_g, gm_b, ws_blk, bias_blk)
    else:
        out_shape = [sds((m, QK_WIDTH), F32), sds((m, QK_WIDTH), F32), sds((m, V_WIDTH), F32)] + tail_shape
        out_specs = [row(QK_WIDTH), row(QK_WIDTH), row(V_WIDTH)] + tail_specs
        in_specs = [row(D_MODEL), _resident((1, D_MODEL)), _resident((D_MODEL, IN_WIDTH))] + consts
        args = (x, pre_g, w_in_b, gm_g, gm_b, ws_blk, bias_blk)
    return pl.pallas_call(
        functools.partial(_inproj_kernel, chunk_len=chunk_len, native_kv=native_kv),
        grid=(m // tm,),
        in_specs=in_specs,
        out_specs=out_specs,
        out_shape=out_shape,
        compiler_params=_params(1),
        name="inproj",
    )(*args)


def _diff_lambda(lamv_ref, lam0):
    lv = lamv_ref[...]
    a = jnp.sum(lv[0:1] * lv[1:2], axis=-1, keepdims=True)
    b = jnp.sum(lv[2:3] * lv[3:4], axis=-1, keepdims=True)
    return jnp.exp(a) - jnp.exp(b) + lam0


def _subln(o, sg, lam0):
    return _rms(o, sg, SUBLN_EPS) * (1.0 - lam0)


def _prompt_attn_kernel(lamv_ref, q_ref, kt_ref, v_ref, sg_ref, o_ref, *, tq, nq, heads, lam0):
    qi = pl.program_id(2)
    lam = _diff_lambda(lamv_ref, lam0)
    lane = lax.broadcasted_iota(jnp.int32, (tq, HEAD_DIM), 1)
    r = lax.broadcasted_iota(jnp.int32, (tq, tq), 0)
    c = lax.broadcasted_iota(jnp.int32, (tq, tq), 1)
    causal = c <= r

    def softmax_pv(qz, hs, kv):
        s = _dot(qz, kt_ref[0, hs, :kv])
        diag = jnp.where(causal, s[:, kv - tq:], NEG_INF)
        s = diag if kv == tq else jnp.concatenate([s[:, :kv - tq], diag], axis=1)
        e = jnp.exp2(s - jnp.max(s, axis=-1, keepdims=True))
        acc = _dot(e.astype(BF16), v_ref[0, :kv, hs])
        return acc / jnp.sum(e, axis=-1, keepdims=True)

    for ci in range(nq):
        @pl.when(qi == ci)
        def _(ci=ci):
            kv = (ci + 1) * tq
            for hd in range(heads):
                hs = slice(hd * HEAD_DIM, (hd + 1) * HEAD_DIM)
                q = q_ref[0, :, hs]
                zero = jnp.zeros_like(q)
                o = (softmax_pv(jnp.where(lane < QK_DIM, q, zero), hs, kv)
                     - lam * softmax_pv(jnp.where(lane >= QK_DIM, q, zero), hs, kv))
                o_ref[0, :, hs] = _subln(o, sg_ref[...], lam0).astype(o_ref.dtype)


def _prompt_attn(lamv, qb, ktb, vb, subln_g, *, lam0, tq, heads):
    b, s, _ = qb.shape
    nq = s // tq
    assert s % tq == 0 and N_HEADS % heads == 0
    w = heads * HEAD_DIM
    return pl.pallas_call(
        functools.partial(_prompt_attn_kernel, tq=tq, nq=nq, heads=heads, lam0=lam0),
        grid=(b, N_HEADS // heads, nq),
        in_specs=[pl.BlockSpec((4, QK_DIM), lambda bi, h, qi: (0, 0)),
                  pl.BlockSpec((1, tq, w), lambda bi, h, qi: (bi, qi, h)),
                  pl.BlockSpec((1, w, s), lambda bi, h, qi: (bi, h, 0)),
                  pl.BlockSpec((1, s, w), lambda bi, h, qi: (bi, 0, h)),
                  pl.BlockSpec((1, HEAD_DIM), lambda bi, h, qi: (0, 0))],
        out_specs=pl.BlockSpec((1, tq, w), lambda bi, h, qi: (bi, qi, h)),
        out_shape=jax.ShapeDtypeStruct((b, s, V_WIDTH), BF16),
        compiler_params=_params(3),
        name="prompt_attn",
    )(lamv, qb, ktb, vb, subln_g)


def _ffn_slice(h, wup, wdn):
    up = jnp.maximum(_dot(h, wup), 0.0)
    return _dot((up * up).astype(BF16), wdn)


def _ffn_kernel(h_ref, wup_ref, wdn_ref, f_ref):
    f_ref[...] = _ffn_slice(h_ref[...], wup_ref[...], wdn_ref[...])


def _ffn(h, w_up_b, w_down_b, *, tm):
    m = h.shape[0]
    assert m % tm == 0
    row = pl.BlockSpec((tm, D_MODEL), lambda i: (i, 0))
    return pl.pallas_call(
        _ffn_kernel,
        grid=(m // tm,),
        in_specs=[row, _resident(w_up_b.shape), _resident(w_down_b.shape)],
        out_specs=row,
        out_shape=jax.ShapeDtypeStruct((m, D_MODEL), F32),
        compiler_params=_params(1),
        name="ffn",
    )(h, w_up_b, w_down_b)


PAGES_PER_STEP = 8
PAGE_SLOTS = 3
EXP_CHUNK = 1024


def _sample_attn_kernel(pt_ref, lamv_ref, q_ref, kn_ref, vn_ref, sg_ref, h_ref, wup_ref, wdn_ref, ck_hbm, cv_hbm,
                        o_ref, f_ref, kbuf, vbuf, ksem, vsem, s_scr, e_scr, m_scr, l_scr, acc_scr,
                        *, n_seq, n_pages, t_new, n_ffn_tiles, page0, lam0):
    P = PAGES_PER_STEP
    r = pl.program_id(0)
    t = pl.program_id(1)
    k_steps = n_pages // P
    n_steps = (n_seq + 1) * k_steps
    past = n_pages * PAGE
    rows = N_HEADS * 2 * t_new
    tsh = t_new.bit_length() - 1
    qsh = QK_DIM.bit_length() - 1
    last = t == k_steps - 1
    has_prev = r >= 1
    has_cur = r < n_seq
    step = r * k_steps + t
    slot = step % PAGE_SLOTS

    def page_group(g, act):
        g_row = g // k_steps
        g_t = g - g_row * k_steps
        g_slot = g % PAGE_SLOTS

        @pl.when(g_row < n_seq)
        def _():
            for i in range(P):
                page = page0 + pt_ref[g_row, g_t * P + i]
                act(pltpu.make_async_copy(ck_hbm.at[page], kbuf.at[g_slot, i], ksem.at[g_slot, i]))

        @pl.when(g_row >= 1)
        def _():
            for i in range(P):
                page = page0 + pt_ref[g_row - 1, g_t * P + i]
                act(pltpu.make_async_copy(cv_hbm.at[page], vbuf.at[g_slot, i], vsem.at[g_slot, i]))

    @pl.when(step == 0)
    def _():
        for g in range(PAGE_SLOTS - 1):
            page_group(jnp.int32(g), lambda cp: cp.start())

    @pl.when(step + (PAGE_SLOTS - 1) < n_steps)
    def _():
        page_group(step + (PAGE_SLOTS - 1), lambda cp: cp.start())

    page_group(step, lambda cp: cp.wait())

    has_ffn = r < n_ffn_tiles

    @pl.when(t == 0)
    def _():
        acc_scr[...] = jnp.zeros(acc_scr.shape, F32)
        m_scr[...] = jnp.full(m_scr.shape, NEG_INF, F32)

    @pl.when(has_ffn & (t == 0))
    def _():
        f_ref[...] = jnp.zeros(f_ref.shape, F32)

    def ffn_step():
        width = D_FF // k_steps
        c0 = pl.multiple_of(t * width, width)
        f_ref[...] += _ffn_slice(h_ref[...], wup_ref[:, pl.ds(c0, width)], wdn_ref[pl.ds(c0, width), :])

    def lane_blocks(x):
        return [x[:, c * LANES:(c + 1) * LANES] for c in range(x.shape[1] // LANES)]

    def pv(e_cols, head_rows):
        outs = []
        for cb in range(N_HEADS // 2):
            e2 = e_scr[cb * 4 * t_new:(cb + 1) * 4 * t_new, e_cols]
            v2 = jnp.concatenate([head_rows(2 * cb), head_rows(2 * cb + 1)], axis=1).astype(BF16)
            r2 = _dot(e2, v2)
            outs += [r2[:2 * t_new, :HEAD_DIM], r2[2 * t_new:, HEAD_DIM:]]
        return jnp.concatenate(outs, axis=1)

    def q_blockdiag():
        q = q_ref[0]
        qt = jnp.concatenate([q] * (rows // t_new), axis=0)
        rr = lax.broadcasted_iota(jnp.int32, qt.shape, 0)
        cc = lax.broadcasted_iota(jnp.int32, qt.shape, 1)
        return jnp.where((cc >> qsh) == (rr >> tsh), qt, 0.0).astype(BF16)

    def pv_step():
        acc = acc_scr[...]
        for i in range(P):
            off = pl.multiple_of((t * P + i) * PAGE, PAGE)
            acc = acc + pv(pl.ds(off, PAGE), lambda hd: vbuf[slot, i, pl.ds(hd, PAGE, stride=N_HEADS), :])
        acc_scr[...] = acc

    def qk_step():
        qbd = q_blockdiag()
        m = m_scr[...]
        for i in range(0, P, 2):
            kt2 = jnp.concatenate([kbuf[slot, i], kbuf[slot, i + 1]], axis=1).astype(BF16)
            s = _dot(qbd, kt2)
            off = pl.multiple_of((t * P + i) * PAGE, 2 * PAGE)
            s_scr[:, pl.ds(off, 2 * PAGE)] = s
            for blk in lane_blocks(s):
                m = jnp.maximum(m, blk)
        m_scr[...] = m

    interior = has_prev & has_cur & has_ffn

    @pl.when(interior)
    def _():
        ffn_step()
        pv_step()
        qk_step()

    @pl.when(jnp.logical_not(interior))
    def _():
        pl.when(has_ffn)(ffn_step)
        pl.when(has_prev)(pv_step)
        pl.when(has_cur)(qk_step)

    @pl.when(has_prev & last)
    def _():
        lam = _diff_lambda(lamv_ref, lam0)
        vn = jnp.concatenate([vn_ref[0], jnp.zeros((PAGE - t_new, V_WIDTH), F32)], axis=0)
        acc = acc_scr[...] + pv(slice(past, past + PAGE), lambda hd: vn[:, hd * HEAD_DIM:(hd + 1) * HEAD_DIM])
        inv = 1.0 / l_scr[...]
        sg = sg_ref[...]
        outs = []
        for h in range(N_HEADS):
            a = acc[:, h * HEAD_DIM:(h + 1) * HEAD_DIM]
            i1 = inv[h * 2 * t_new:h * 2 * t_new + t_new]
            i2 = inv[h * 2 * t_new + t_new:(h + 1) * 2 * t_new]
            outs.append(_subln(a[:t_new] * i1 - lam * (a[t_new:] * i2), sg, lam0))
        o_ref[0] = jnp.concatenate(outs, axis=1)

    @pl.when(has_cur & last)
    def _():
        qbd = q_blockdiag()
        kn = jnp.concatenate([kn_ref[0], jnp.zeros((PAGE - t_new, QK_WIDTH), F32)], axis=0).astype(BF16)
        s = _nt_dot(qbd, kn)
        rr = lax.broadcasted_iota(jnp.int32, s.shape, 0)
        cc = lax.broadcasted_iota(jnp.int32, s.shape, 1)
        s = jnp.where(cc <= (rr & (t_new - 1)), s, NEG_INF)
        s_scr[:, past:past + PAGE] = s
        m = jnp.max(jnp.maximum(m_scr[...], s), axis=-1, keepdims=True)
        mb = jnp.broadcast_to(m, (rows, LANES))

        def exp_cols(cols, width, l):
            e = jnp.exp2(s_scr[:, cols] - jnp.concatenate([mb] * (width // LANES), axis=1))
            e_scr[:, cols] = e.astype(BF16)
            for blk in lane_blocks(e):
                l = l + blk
            return l

        l = lax.fori_loop(
            0, past // EXP_CHUNK,
            lambda j, l: exp_cols(pl.ds(pl.multiple_of(j * EXP_CHUNK, EXP_CHUNK), EXP_CHUNK), EXP_CHUNK, l),
            jnp.zeros((rows, LANES), F32))
        l = exp_cols(slice(past, past + PAGE), PAGE, l)
        l_scr[...] = jnp.sum(l, axis=-1, keepdims=True)


def _sample_attn_ffn(page_table, lamv, q, k_new, v_new, subln_g, cache_kt, cache_vr, h_ffn, w_up_b, w_down_b,
                     *, page0, lam0, tm):
    n, t_new, _ = q.shape
    n_pages = page_table.shape[1]
    P = PAGES_PER_STEP
    rows = N_HEADS * 2 * t_new
    past = n_pages * PAGE
    assert n_pages % P == 0 and P % 2 == 0 and past % EXP_CHUNK == 0
    assert t_new & (t_new - 1) == 0 and t_new % 8 == 0
    k_steps = n_pages // P
    m_ffn = h_ffn.shape[0]
    n_ffn_tiles = m_ffn // tm
    assert m_ffn % tm == 0 and n_ffn_tiles <= n + 1 and D_FF % k_steps == 0 and (D_FF // k_steps) % LANES == 0
    assert (n + 1) * k_steps >= PAGE_SLOTS - 1

    def cur(ri):
        return jnp.minimum(ri, n - 1)

    def prev(ri):
        return jnp.maximum(ri - 1, 0)

    tok_cur = pl.BlockSpec((1, t_new, QK_WIDTH), lambda ri, t, pt: (cur(ri), 0, 0))
    tok_prev = pl.BlockSpec((1, t_new, V_WIDTH), lambda ri, t, pt: (prev(ri), 0, 0))
    ffn_row = pl.BlockSpec((tm, D_MODEL), lambda ri, t, pt: (jnp.minimum(ri, n_ffn_tiles - 1), 0))
    hbm = pl.BlockSpec(memory_space=pl.ANY)
    grid_spec = pltpu.PrefetchScalarGridSpec(
        num_scalar_prefetch=1,
        grid=(n + 1, k_steps),
        in_specs=[pl.BlockSpec((4, QK_DIM), lambda ri, t, pt: (0, 0)), tok_cur, tok_cur, tok_prev,
                  pl.BlockSpec((1, HEAD_DIM), lambda ri, t, pt: (0, 0)),
                  ffn_row, _resident(w_up_b.shape), _resident(w_down_b.shape), hbm, hbm],
        out_specs=[tok_prev, ffn_row],
        scratch_shapes=[pltpu.VMEM((PAGE_SLOTS, P, QK_WIDTH, PAGE), F32),
                        pltpu.VMEM((PAGE_SLOTS, P, PAGE * N_HEADS, HEAD_DIM), F32),
                        pltpu.SemaphoreType.DMA((PAGE_SLOTS, P)),
                        pltpu.SemaphoreType.DMA((PAGE_SLOTS, P)),
                        pltpu.VMEM((rows, past + PAGE), F32),
                        pltpu.VMEM((rows, past + PAGE), BF16),
                        pltpu.VMEM((rows, LANES), F32),
                        pltpu.VMEM((rows, 1), F32),
                        pltpu.VMEM((2 * t_new, V_WIDTH), F32)],
    )
    return pl.pallas_call(
        functools.partial(_sample_attn_kernel, n_seq=n, n_pages=n_pages, t_new=t_new,
                          n_ffn_tiles=n_ffn_tiles, page0=page0, lam0=lam0),
        grid_spec=grid_spec,
        out_shape=[jax.ShapeDtypeStruct((n, t_new, V_WIDTH), F32), jax.ShapeDtypeStruct((m_ffn, D_MODEL), F32)],
        compiler_params=_params(2),
        name="sample_attn_ffn",
    )(page_table, lamv, q, k_new, v_new, subln_g, h_ffn, w_up_b, w_down_b, cache_kt, cache_vr)


def _mix_kernel(x_ref, oa_ref, ogm_ref, gates_ref, wba_ref, wbg_ref, wo_ref, g_mix_ref, g_pre_ref, x1_ref, h_ref):
    gates = gates_ref[...]
    merged = (gates[:, :D_MODEL] * _dot(oa_ref[...].astype(BF16), wba_ref[...])
              + gates[:, D_MODEL:] * _dot(ogm_ref[...], wbg_ref[...]))
    x = x_ref[...] + _rms(_dot(merged.astype(BF16), wo_ref[...]), g_mix_ref[...], EPS)
    x1_ref[...] = x
    h_ref[...] = _rms(x, g_pre_ref[...], EPS).astype(BF16)


def _mix(x, oa, ogm, gates, wts, gains, *, tm):
    m = x.shape[0]
    assert m % tm == 0
    row = lambda w: pl.BlockSpec((tm, w), lambda i: (i, 0))
    return pl.pallas_call(
        _mix_kernel,
        grid=(m // tm,),
        in_specs=[row(D_MODEL), row(V_WIDTH), row(GM_WIDTH), row(2 * D_MODEL)]
                 + [_resident(w.shape) for w in wts] + [_resident(g.shape) for g in gains],
        out_specs=[row(D_MODEL), row(D_MODEL)],
        out_shape=[jax.ShapeDtypeStruct((m, D_MODEL), F32), jax.ShapeDtypeStruct((m, D_MODEL), BF16)],
        compiler_params=_params(1),
        name="mix",
    )(x, oa, ogm, gates, *wts, *gains)


def _out_kernel(x1_ref, f_ref, pe_ref, wpg_ref, wpe_ref, g_ffn_ref, g_pe_ref, y_ref):
    x = x1_ref[...] + _rms(f_ref[...], g_ffn_ref[...], EPS)
    gate = jax.nn.sigmoid(_dot(x.astype(BF16), wpg_ref[...]))
    y_ref[...] = x + _rms(gate * _dot(pe_ref[...].astype(BF16), wpe_ref[...]), g_pe_ref[...], EPS)


def _out(x1, f, pe, wts, gains, *, tm):
    m = x1.shape[0]
    assert m % tm == 0
    row = lambda w: pl.BlockSpec((tm, w), lambda i: (i, 0))
    return pl.pallas_call(
        _out_kernel,
        grid=(m // tm,),
        in_specs=[row(D_MODEL), row(D_MODEL), row(PLE_DIM)]
                 + [_resident(w.shape) for w in wts] + [_resident(g.shape) for g in gains],
        out_specs=row(D_MODEL),
        out_shape=jax.ShapeDtypeStruct((m, D_MODEL), F32),
        compiler_params=_params(1),
        name="out",
    )(x1, f, pe, *wts, *gains)


def _gmlp_blocks(ws, bs, chunk_len):
    rep = GM_CHUNK // chunk_len
    ws_blk = jnp.tile(ws[:, :chunk_len, :chunk_len], (1, rep, rep))
    bias = jnp.tile(jnp.transpose(bs[:, :chunk_len]), (rep, 1))
    return ws_blk, jnp.repeat(bias, GM_GROUP_DIM, axis=1)


def kernel(x_prompt, x_sample, cache_k, cache_v, page_table, p_prompt, p_sample, pre_mix_g, w_in, lam_q1, lam_k1, lam_q2, lam_k2, subln_g, gm_norm_g, gm_norm_b, gm_ws, gm_bs, w_branch_attn, w_branch_gm, w_o, post_mix_g, pre_ffn_g, w_up, w_down, post_ffn_g, w_pe, w_pg, pe_post_g):
    depth, n_pool = cache_k.shape[0], cache_k.shape[1]
    assert depth == 1
    b, s, _ = x_prompt.shape
    n, t_new, _ = x_sample.shape
    i = 0
    tm = 256
    tm_wide = 512
    lam0 = _lambda_init(i)
    row2 = lambda a: a[i].reshape(1, -1)
    bf = lambda w: w[i].astype(BF16)
    w_in_b = bf(w_in)
    wkt_b = jnp.transpose(w_in[i][:, _K0:_V0]).astype(BF16)
    mix_w = (bf(w_branch_attn), bf(w_branch_gm), bf(w_o))
    mix_g = (row2(post_mix_g), row2(pre_ffn_g))
    w_up_b, w_down_b = bf(w_up), bf(w_down)
    out_w = (bf(w_pg), bf(w_pe))
    out_g = (row2(post_ffn_g), row2(pe_post_g))
    lamv = jnp.stack([lam_q1[i], lam_k1[i], lam_q2[i], lam_k2[i]])
    sg = row2(subln_g)
    norm_args = (row2(gm_norm_g), row2(gm_norm_b))

    xp = x_prompt.reshape(b * s, D_MODEL)
    chunk = min(s, GM_CHUNK)
    qb, kt, ktb, v_rows, vb, gvn_p, ogm, gates = _inproj(
        xp, row2(pre_mix_g), w_in_b, wkt_b, *norm_args, *_gmlp_blocks(gm_ws[i], gm_bs[i], chunk),
        chunk_len=chunk, seq_len=s, tm=tm_wide)
    oa = _prompt_attn(lamv, qb.reshape(b, s, -1), ktb, vb.reshape(b, s, -1), sg, lam0=lam0, tq=256, heads=2)
    x1_p, h_p = _mix(xp, oa.reshape(b * s, -1), ogm, gates, mix_w, mix_g, tm=tm_wide)

    xs = x_sample.reshape(n * t_new, D_MODEL)
    chunk = min(t_new, GM_CHUNK)
    q, k_s, v_s, gvn_s, ogm, gates = _inproj(
        xs, row2(pre_mix_g), w_in_b, None, *norm_args, *_gmlp_blocks(gm_ws[i], gm_bs[i], chunk),
        chunk_len=chunk, seq_len=t_new, tm=tm)
    cache_kt = jnp.transpose(cache_k, (0, 1, 3, 4, 5, 2)).reshape(depth * n_pool, QK_WIDTH, PAGE)
    cache_vr = cache_v.reshape(depth * n_pool, PAGE * N_HEADS, HEAD_DIM)
    oa, f_p = _sample_attn_ffn(page_table, lamv, q.reshape(n, t_new, -1), k_s.reshape(n, t_new, -1),
                               v_s.reshape(n, t_new, -1), sg, cache_kt, cache_vr, h_p, w_up_b, w_down_b,
                               page0=i * n_pool, lam0=lam0, tm=tm)

    yp = _out(x1_p, f_p, p_prompt[i].reshape(b * s, -1), out_w, out_g, tm=tm_wide)
    x1_s, h_s = _mix(xs, oa.reshape(n * t_new, -1), ogm, gates, mix_w, mix_g, tm=tm)
    ys = _out(x1_s, _ffn(h_s, w_up_b, w_down_b, tm=tm), p_sample[i].reshape(n * t_new, -1), out_w, out_g, tm=tm)

    k_prompt = jnp.transpose(kt.reshape(1, b, N_HEADS, 2, QK_DIM, s), (0, 1, 5, 2, 3, 4))
    return (yp.reshape(b, s, D_MODEL), ys.reshape(n, t_new, D_MODEL),
            k_prompt, v_rows.reshape(1, b, s, N_HEADS, HEAD_DIM), gvn_p.reshape(1, b, s, GM_WIDTH),
            k_s.reshape(1, n, t_new, N_HEADS, 2, QK_DIM), v_s.reshape(1, n, t_new, N_HEADS, HEAD_DIM),
            gvn_s.reshape(1, n, t_new, GM_WIDTH))
```

```python
import functools
import math

import jax
import jax.numpy as jnp
from jax import lax
from jax.experimental import pallas as pl
from jax.experimental.pallas import tpu as pltpu

F32 = jnp.float32
BF16 = jnp.bfloat16

D_MODEL = 1024
N_HEADS = 8
QK_DIM = 64
HEAD_DIM = 2 * QK_DIM
QK_WIDTH = N_HEADS * HEAD_DIM
V_WIDTH = N_HEADS * HEAD_DIM
GM_GROUPS = 4
GM_CHUNK = 128
GM_GROUP_DIM = 128
GM_WIDTH = GM_GROUPS * GM_GROUP_DIM
D_FF = 4 * D_MODEL
PLE_DIM = 256
PAGE = 128
EPS = 1e-6
SUBLN_EPS = 1e-5
NEG_INF = -1e30
Q_PRESCALE = QK_DIM ** -0.5 * math.log2(math.e)

VMEM_LIMIT_BYTES = 56 * 1024 * 1024
FUSED_VMEM_LIMIT_BYTES = 60 * 1024 * 1024
LANES = 128

_Q0, _K0, _V0 = 0, QK_WIDTH, 2 * QK_WIDTH
_U0 = _V0 + V_WIDTH
_GV0 = _U0 + GM_WIDTH
_G0 = _GV0 + GM_WIDTH
IN_WIDTH = _G0 + 2 * D_MODEL


def _lambda_init(layer_idx):
    return 0.8 - 0.6 * math.exp(-0.3 * layer_idx)


def _rms(x, g, eps):
    return x * lax.rsqrt(jnp.mean(x * x, axis=-1, keepdims=True) + eps) * g


def _dot(a, b):
    return jnp.dot(a, b, preferred_element_type=F32)


def _nt_dot(a, b):
    return lax.dot_general(a, b, (((1,), (1,)), ((), ())), preferred_element_type=F32)


def _resident(shape):
    nd = len(shape)
    return pl.BlockSpec(shape, lambda *_: (0,) * nd, pipeline_mode=pl.Buffered(1))


def _params(n_grid_axes):
    return pltpu.CompilerParams(dimension_semantics=("arbitrary",) * n_grid_axes,
                                vmem_limit_bytes=VMEM_LIMIT_BYTES)


def _inproj_kernel(*refs, chunk_len, native_kv):
    if native_kv:
        (x_ref, g_ref, w_ref, wkt_ref, gmg_ref, gmb_ref, ws_ref, bias_ref,
         q_ref, kt_ref, ktb_ref, v_ref, vb_ref, gvn_ref, ogm_ref, gates_ref) = refs
    else:
        (x_ref, g_ref, w_ref, gmg_ref, gmb_ref, ws_ref, bias_ref,
         q_ref, k_ref, v_ref, gvn_ref, ogm_ref, gates_ref) = refs
    tm = x_ref.shape[0]
    h = _rms(x_ref[...], g_ref[...], EPS).astype(BF16)

    def seg(a, b):
        return _dot(h, w_ref[:, a:b])

    q_ref[...] = (seg(_Q0, _K0) * Q_PRESCALE).astype(q_ref.dtype)
    zv = seg(_V0, _U0)
    if native_kv:
        zkt = _nt_dot(wkt_ref[...], h)
        kt_ref[0] = zkt
        ktb_ref[0] = zkt.astype(BF16)
        for hd in range(N_HEADS):
            v_ref[pl.ds(hd, tm, stride=N_HEADS), :] = zv[:, hd * HEAD_DIM:(hd + 1) * HEAD_DIM]
        vb_ref[...] = zv.astype(BF16)
    else:
        k_ref[...] = seg(_K0, _V0)
        v_ref[...] = zv
    gates_ref[...] = jax.nn.sigmoid(seg(_G0, IN_WIDTH))

    u = jax.nn.gelu(seg(_U0, _GV0))
    gv = jax.nn.gelu(seg(_GV0, _G0))
    mu = jnp.mean(gv, axis=-1, keepdims=True)
    xc = gv - mu
    vn = xc * lax.rsqrt(jnp.mean(xc * xc, axis=-1, keepdims=True) + EPS) * gmg_ref[...] + gmb_ref[...]
    gvn_ref[...] = vn
    vnb = vn.astype(BF16)

    row = lax.broadcasted_iota(jnp.int32, (GM_CHUNK, GM_CHUNK), 0)
    col = lax.broadcasted_iota(jnp.int32, (GM_CHUNK, GM_CHUNK), 1)
    sh = chunk_len.bit_length() - 1
    keep = (col <= row) & ((row >> sh) == (col >> sh))
    for g in range(GM_GROUPS):
        wm = jnp.where(keep, ws_ref[g], 0.0).astype(BF16)
        cs = slice(g * GM_GROUP_DIM, (g + 1) * GM_GROUP_DIM)
        for c in range(tm // GM_CHUNK):
            rs = slice(c * GM_CHUNK, (c + 1) * GM_CHUNK)
            mixed = _dot(wm, vnb[rs, cs]) + bias_ref[:, cs]
            ogm_ref[rs, cs] = (u[rs, cs] * mixed).astype(BF16)


def _inproj(x, pre_g, w_in_b, wkt_b, gm_g, gm_b, ws_blk, bias_blk, *, chunk_len, seq_len, tm):
    m = x.shape[0]
    native_kv = wkt_b is not None
    assert m % tm == 0 and tm % GM_CHUNK == 0
    assert chunk_len & (chunk_len - 1) == 0 and GM_CHUNK % chunk_len == 0
    row = lambda w: pl.BlockSpec((tm, w), lambda i: (i, 0))
    sds = jax.ShapeDtypeStruct
    tail_shape = [sds((m, GM_WIDTH), F32), sds((m, GM_WIDTH), BF16), sds((m, 2 * D_MODEL), F32)]
    tail_specs = [row(GM_WIDTH), row(GM_WIDTH), row(2 * D_MODEL)]
    consts = [_resident((1, GM_WIDTH)), _resident((1, GM_WIDTH)),
              _resident((GM_GROUPS, GM_CHUNK, GM_CHUNK)), _resident((GM_CHUNK, GM_WIDTH))]
    if native_kv:
        assert seq_len % tm == 0
        spb = seq_len // tm
        nb = m // seq_len
        kt_spec = pl.BlockSpec((1, QK_WIDTH, tm), lambda i: (i // spb, 0, i % spb))
        out_shape = [sds((m, QK_WIDTH), BF16), sds((nb, QK_WIDTH, seq_len), F32), sds((nb, QK_WIDTH, seq_len), BF16),
                     sds((m * N_HEADS, HEAD_DIM), F32), sds((m, V_WIDTH), BF16)] + tail_shape
        out_specs = [row(QK_WIDTH), kt_spec, kt_spec,
                     pl.BlockSpec((tm * N_HEADS, HEAD_DIM), lambda i: (i, 0)), row(V_WIDTH)] + tail_specs
        in_specs = [row(D_MODEL), _resident((1, D_MODEL)), _resident((D_MODEL, IN_WIDTH)),
                    _resident((QK_WIDTH, D_MODEL))] + consts
        args = (x, pre_g, w_in_b, wkt_b, gm_g, gm_b, ws_blk, bias_blk)
    else:
        out_shape = [sds((m, QK_WIDTH), F32), sds((m, QK_WIDTH), F32), sds((m, V_WIDTH), F32)] + tail_shape
        out_specs = [row(QK_WIDTH), row(QK_WIDTH), row(V_WIDTH)] + tail_specs
        in_specs = [row(D_MODEL), _resident((1, D_MODEL)), _resident((D_MODEL, IN_WIDTH))] + consts
        args = (x, pre_g, w_in_b, gm_g, gm_b, ws_blk, bias_blk)
    return pl.pallas_call(
        functools.partial(_inproj_kernel, chunk_len=chunk_len, native_kv=native_kv),
        grid=(m // tm,),
        in_specs=in_specs,
        out_specs=out_specs,
        out_shape=out_shape,
        compiler_params=_params(1),
        name="inproj",
    )(*args)


def _diff_lambda(lamv_ref, lam0):
    lv = lamv_ref[...]
    a = jnp.sum(lv[0:1] * lv[1:2], axis=-1, keepdims=True)
    b = jnp.sum(lv[2:3] * lv[3:4], axis=-1, keepdims=True)
    return jnp.exp(a) - jnp.exp(b) + lam0


def _subln(o, sg, lam0):
    return _rms(o, sg, SUBLN_EPS) * (1.0 - lam0)


def _prompt_attn_kernel(lamv_ref, q_ref, kt_ref, v_ref, sg_ref, o_ref, *, tq, nq, heads, lam0):
    qi = pl.program_id(2)
    lam = _diff_lambda(lamv_ref, lam0)
    lane = lax.broadcasted_iota(jnp.int32, (tq, HEAD_DIM), 1)
    r = lax.broadcasted_iota(jnp.int32, (tq, tq), 0)
    c = lax.broadcasted_iota(jnp.int32, (tq, tq), 1)
    causal = c <= r

    def softmax_pv(qz, hs, kv):
        s = _dot(qz, kt_ref[0, hs, :kv])
        diag = jnp.where(causal, s[:, kv - tq:], NEG_INF)
        s = diag if kv == tq else jnp.concatenate([s[:, :kv - tq], diag], axis=1)
        e = jnp.exp2(s - jnp.max(s, axis=-1, keepdims=True))
        acc = _dot(e.astype(BF16), v_ref[0, :kv, hs])
        return acc / jnp.sum(e, axis=-1, keepdims=True)

    for ci in range(nq):
        @pl.when(qi == ci)
        def _(ci=ci):
            kv = (ci + 1) * tq
            for hd in range(heads):
                hs = slice(hd * HEAD_DIM, (hd + 1) * HEAD_DIM)
                q = q_ref[0, :, hs]
                zero = jnp.zeros_like(q)
                o = (softmax_pv(jnp.where(lane < QK_DIM, q, zero), hs, kv)
                     - lam * softmax_pv(jnp.where(lane >= QK_DIM, q, zero), hs, kv))
                o_ref[0, :, hs] = _subln(o, sg_ref[...], lam0).astype(o_ref.dtype)


def _prompt_attn(lamv, qb, ktb, vb, subln_g, *, lam0, tq, heads):
    b, s, _ = qb.shape
    nq = s // tq
    assert s % tq == 0 and N_HEADS % heads == 0
    w = heads * HEAD_DIM
    return pl.pallas_call(
        functools.partial(_prompt_attn_kernel, tq=tq, nq=nq, heads=heads, lam0=lam0),
        grid=(b, N_HEADS // heads, nq),
        in_specs=[pl.BlockSpec((4, QK_DIM), lambda bi, h, qi: (0, 0)),
                  pl.BlockSpec((1, tq, w), lambda bi, h, qi: (bi, qi, h)),
                  pl.BlockSpec((1, w, s), lambda bi, h, qi: (bi, h, 0)),
                  pl.BlockSpec((1, s, w), lambda bi, h, qi: (bi, 0, h)),
                  pl.BlockSpec((1, HEAD_DIM), lambda bi, h, qi: (0, 0))],
        out_specs=pl.BlockSpec((1, tq, w), lambda bi, h, qi: (bi, qi, h)),
        out_shape=jax.ShapeDtypeStruct((b, s, V_WIDTH), BF16),
        compiler_params=_params(3),
        name="prompt_attn",
    )(lamv, qb, ktb, vb, subln_g)


def _ffn_slice(h, wup, wdn):
    up = jnp.maximum(_dot(h, wup), 0.0)
    return _dot((up * up).astype(BF16), wdn)


def _out_stage(x1, f, pe, wpg, wpe, g_ffn, g_pe):
    x = x1 + _rms(f, g_ffn, EPS)
    gate = jax.nn.sigmoid(_dot(x.astype(BF16), wpg))
    return x + _rms(gate * _dot(pe.astype(BF16), wpe), g_pe, EPS)


def _ffn_kernel(h_ref, wup_ref, wdn_ref, f_ref):
    f_ref[...] = _ffn_slice(h_ref[...], wup_ref[...], wdn_ref[...])


def _ffn(h, w_up_b, w_down_b, *, tm):
    m = h.shape[0]
    assert m % tm == 0
    row = pl.BlockSpec((tm, D_MODEL), lambda i: (i, 0))
    return pl.pallas_call(
        _ffn_kernel,
        grid=(m // tm,),
        in_specs=[row, _resident(w_up_b.shape), _resident(w_down_b.shape)],
        out_specs=row,
        out_shape=jax.ShapeDtypeStruct((m, D_MODEL), F32),
        compiler_params=_params(1),
        name="ffn",
    )(h, w_up_b, w_down_b)


PAGES_PER_STEP = 8
PAGE_SLOTS = 3
EXP_CHUNK = 1024


def _sample_attn_kernel(pt_ref, lamv_ref, q_ref, kn_ref, vn_ref, sg_ref, h_ref, wup_ref, wdn_ref,
                        x1_ref, pe_ref, wpg_ref, wpe_ref, g_ffn_ref, g_pe_ref, ck_hbm, cv_hbm,
                        o_ref, y_ref, kbuf, vbuf, ksem, vsem, s_scr, e_scr, m_scr, l_scr, acc_scr, f_ref,
                        *, n_seq, n_pages, t_new, n_ffn_tiles, page0, lam0):
    P = PAGES_PER_STEP
    r = pl.program_id(0)
    t = pl.program_id(1)
    k_steps = n_pages // P
    n_steps = (n_seq + 1) * k_steps
    past = n_pages * PAGE
    rows = N_HEADS * 2 * t_new
    tsh = t_new.bit_length() - 1
    qsh = QK_DIM.bit_length() - 1
    last = t == k_steps - 1
    has_prev = r >= 1
    has_cur = r < n_seq
    step = r * k_steps + t
    slot = step % PAGE_SLOTS

    def page_group(g, act):
        g_row = g // k_steps
        g_t = g - g_row * k_steps
        g_slot = g % PAGE_SLOTS

        @pl.when(g_row < n_seq)
        def _():
            for i in range(P):
                page = page0 + pt_ref[g_row, g_t * P + i]
                act(pltpu.make_async_copy(ck_hbm.at[page], kbuf.at[g_slot, i], ksem.at[g_slot, i]))

        @pl.when(g_row >= 1)
        def _():
            for i in range(P):
                page = page0 + pt_ref[g_row - 1, g_t * P + i]
                act(pltpu.make_async_copy(cv_hbm.at[page], vbuf.at[g_slot, i], vsem.at[g_slot, i]))

    @pl.when(step == 0)
    def _():
        for g in range(PAGE_SLOTS - 1):
            page_group(jnp.int32(g), lambda cp: cp.start())

    @pl.when(step + (PAGE_SLOTS - 1) < n_steps)
    def _():
        page_group(step + (PAGE_SLOTS - 1), lambda cp: cp.start())

    page_group(step, lambda cp: cp.wait())

    has_ffn = r < n_ffn_tiles

    @pl.when(t == 0)
    def _():
        acc_scr[...] = jnp.zeros(acc_scr.shape, F32)
        m_scr[...] = jnp.full(m_scr.shape, NEG_INF, F32)

    @pl.when(has_ffn & (t == 0))
    def _():
        f_ref[...] = jnp.zeros(f_ref.shape, F32)

    def ffn_step():
        width = D_FF // k_steps
        c0 = pl.multiple_of(t * width, width)
        f_ref[...] += _ffn_slice(h_ref[...], wup_ref[:, pl.ds(c0, width)], wdn_ref[pl.ds(c0, width), :])

    def lane_blocks(x):
        return [x[:, c * LANES:(c + 1) * LANES] for c in range(x.shape[1] // LANES)]

    def pv(e_cols, head_rows):
        outs = []
        for cb in range(N_HEADS // 2):
            e2 = e_scr[cb * 4 * t_new:(cb + 1) * 4 * t_new, e_cols]
            v2 = jnp.concatenate([head_rows(2 * cb), head_rows(2 * cb + 1)], axis=1).astype(BF16)
            r2 = _dot(e2, v2)
            outs += [r2[:2 * t_new, :HEAD_DIM], r2[2 * t_new:, HEAD_DIM:]]
        return jnp.concatenate(outs, axis=1)

    def q_blockdiag():
        q = q_ref[0]
        qt = jnp.concatenate([q] * (rows // t_new), axis=0)
        rr = lax.broadcasted_iota(jnp.int32, qt.shape, 0)
        cc = lax.broadcasted_iota(jnp.int32, qt.shape, 1)
        return jnp.where((cc >> qsh) == (rr >> tsh), qt, 0.0).astype(BF16)

    def pv_step():
        acc = acc_scr[...]
        for i in range(P):
            off = pl.multiple_of((t * P + i) * PAGE, PAGE)
            acc = acc + pv(pl.ds(off, PAGE), lambda hd: vbuf[slot, i, pl.ds(hd, PAGE, stride=N_HEADS), :])
        acc_scr[...] = acc

    def qk_step():
        qbd = q_blockdiag()
        m = m_scr[...]
        for i in range(0, P, 2):
            kt2 = jnp.concatenate([kbuf[slot, i], kbuf[slot, i + 1]], axis=1).astype(BF16)
            s = _dot(qbd, kt2)
            off = pl.multiple_of((t * P + i) * PAGE, 2 * PAGE)
            s_scr[:, pl.ds(off, 2 * PAGE)] = s
            for blk in lane_blocks(s):
                m = jnp.maximum(m, blk)
        m_scr[...] = m

    interior = has_prev & has_cur & has_ffn

    @pl.when(interior)
    def _():
        ffn_step()
        pv_step()
        qk_step()

    @pl.when(jnp.logical_not(interior))
    def _():
        pl.when(has_ffn)(ffn_step)
        pl.when(has_prev)(pv_step)
        pl.when(has_cur)(qk_step)

    @pl.when(has_ffn & last)
    def _():
        y_ref[...] = _out_stage(x1_ref[...], f_ref[...], pe_ref[...], wpg_ref[...], wpe_ref[...],
                                g_ffn_ref[...], g_pe_ref[...])

    @pl.when(has_prev & last)
    def _():
        lam = _diff_lambda(lamv_ref, lam0)
        vn = jnp.concatenate([vn_ref[0], jnp.zeros((PAGE - t_new, V_WIDTH), F32)], axis=0)
        acc = acc_scr[...] + pv(slice(past, past + PAGE), lambda hd: vn[:, hd * HEAD_DIM:(hd + 1) * HEAD_DIM])
        inv = 1.0 / l_scr[...]
        sg = sg_ref[...]
        outs = []
        for h in range(N_HEADS):
            a = acc[:, h * HEAD_DIM:(h + 1) * HEAD_DIM]
            i1 = inv[h * 2 * t_new:h * 2 * t_new + t_new]
            i2 = inv[h * 2 * t_new + t_new:(h + 1) * 2 * t_new]
            outs.append(_subln(a[:t_new] * i1 - lam * (a[t_new:] * i2), sg, lam0))
        o_ref[0] = jnp.concatenate(outs, axis=1)

    @pl.when(has_cur & last)
    def _():
        qbd = q_blockdiag()
        kn = jnp.concatenate([kn_ref[0], jnp.zeros((PAGE - t_new, QK_WIDTH), F32)], axis=0).astype(BF16)
        s = _nt_dot(qbd, kn)
        rr = lax.broadcasted_iota(jnp.int32, s.shape, 0)
        cc = lax.broadcasted_iota(jnp.int32, s.shape, 1)
        s = jnp.where(cc <= (rr & (t_new - 1)), s, NEG_INF)
        s_scr[:, past:past + PAGE] = s
        m = jnp.max(jnp.maximum(m_scr[...], s), axis=-1, keepdims=True)
        mb = jnp.broadcast_to(m, (rows, LANES))

        def exp_cols(cols, width, l):
            e = jnp.exp2(s_scr[:, cols] - jnp.concatenate([mb] * (width // LANES), axis=1))
            e_scr[:, cols] = e.astype(BF16)
            for blk in lane_blocks(e):
                l = l + blk
            return l

        l = lax.fori_loop(
            0, past // EXP_CHUNK,
            lambda j, l: exp_cols(pl.ds(pl.multiple_of(j * EXP_CHUNK, EXP_CHUNK), EXP_CHUNK), EXP_CHUNK, l),
            jnp.zeros((rows, LANES), F32))
        l = exp_cols(slice(past, past + PAGE), PAGE, l)
        l_scr[...] = jnp.sum(l, axis=-1, keepdims=True)


def _sample_attn_ffn(page_table, lamv, q, k_new, v_new, subln_g, cache_kt, cache_vr,
                     h_ffn, w_up_b, w_down_b, x1_ffn, pe_ffn, out_w, out_g, *, page0, lam0, tm):
    n, t_new, _ = q.shape
    n_pages = page_table.shape[1]
    P = PAGES_PER_STEP
    rows = N_HEADS * 2 * t_new
    past = n_pages * PAGE
    assert n_pages % P == 0 and P % 2 == 0 and past % EXP_CHUNK == 0
    assert t_new & (t_new - 1) == 0 and t_new % 8 == 0
    k_steps = n_pages // P
    m_ffn = h_ffn.shape[0]
    n_ffn_tiles = m_ffn // tm
    assert m_ffn % tm == 0 and n_ffn_tiles <= n + 1 and D_FF % k_steps == 0 and (D_FF // k_steps) % LANES == 0
    assert (n + 1) * k_steps >= PAGE_SLOTS - 1

    def cur(ri):
        return jnp.minimum(ri, n - 1)

    def prev(ri):
        return jnp.maximum(ri - 1, 0)

    tok_cur = pl.BlockSpec((1, t_new, QK_WIDTH), lambda ri, t, pt: (cur(ri), 0, 0))
    tok_prev = pl.BlockSpec((1, t_new, V_WIDTH), lambda ri, t, pt: (prev(ri), 0, 0))
    ffn_tile = lambda w: pl.BlockSpec((tm, w), lambda ri, t, pt: (jnp.minimum(ri, n_ffn_tiles - 1), 0))
    hbm = pl.BlockSpec(memory_space=pl.ANY)
    grid_spec = pltpu.PrefetchScalarGridSpec(
        num_scalar_prefetch=1,
        grid=(n + 1, k_steps),
        in_specs=[pl.BlockSpec((4, QK_DIM), lambda ri, t, pt: (0, 0)), tok_cur, tok_cur, tok_prev,
                  pl.BlockSpec((1, HEAD_DIM), lambda ri, t, pt: (0, 0)),
                  ffn_tile(D_MODEL), _resident(w_up_b.shape), _resident(w_down_b.shape),
                  ffn_tile(D_MODEL), ffn_tile(PLE_DIM)]
                 + [_resident(a.shape) for a in (*out_w, *out_g)] + [hbm, hbm],
        out_specs=[tok_prev, ffn_tile(D_MODEL)],
        scratch_shapes=[pltpu.VMEM((PAGE_SLOTS, P, QK_WIDTH, PAGE), F32),
                        pltpu.VMEM((PAGE_SLOTS, P, PAGE * N_HEADS, HEAD_DIM), F32),
                        pltpu.SemaphoreType.DMA((PAGE_SLOTS, P)),
                        pltpu.SemaphoreType.DMA((PAGE_SLOTS, P)),
                        pltpu.VMEM((rows, past + PAGE), F32),
                        pltpu.VMEM((rows, past + PAGE), BF16),
                        pltpu.VMEM((rows, LANES), F32),
                        pltpu.VMEM((rows, 1), F32),
                        pltpu.VMEM((2 * t_new, V_WIDTH), F32),
                        pltpu.VMEM((tm, D_MODEL), F32)],
    )
    return pl.pallas_call(
        functools.partial(_sample_attn_kernel, n_seq=n, n_pages=n_pages, t_new=t_new,
                          n_ffn_tiles=n_ffn_tiles, page0=page0, lam0=lam0),
        grid_spec=grid_spec,
        out_shape=[jax.ShapeDtypeStruct((n, t_new, V_WIDTH), F32), jax.ShapeDtypeStruct((m_ffn, D_MODEL), F32)],
        compiler_params=pltpu.CompilerParams(dimension_semantics=("arbitrary", "arbitrary"),
                                             vmem_limit_bytes=FUSED_VMEM_LIMIT_BYTES),
        name="sample_attn_ffn",
    )(page_table, lamv, q, k_new, v_new, subln_g, h_ffn, w_up_b, w_down_b, x1_ffn, pe_ffn, *out_w, *out_g,
      cache_kt, cache_vr)


def _mix_kernel(x_ref, oa_ref, ogm_ref, gates_ref, wba_ref, wbg_ref, wo_ref, g_mix_ref, g_pre_ref, x1_ref, h_ref):
    gates = gates_ref[...]
    merged = (gates[:, :D_MODEL] * _dot(oa_ref[...].astype(BF16), wba_ref[...])
              + gates[:, D_MODEL:] * _dot(ogm_ref[...], wbg_ref[...]))
    x = x_ref[...] + _rms(_dot(merged.astype(BF16), wo_ref[...]), g_mix_ref[...], EPS)
    x1_ref[...] = x
    h_ref[...] = _rms(x, g_pre_ref[...], EPS).astype(BF16)


def _mix(x, oa, ogm, gates, wts, gains, *, tm):
    m = x.shape[0]
    assert m % tm == 0
    row = lambda w: pl.BlockSpec((tm, w), lambda i: (i, 0))
    return pl.pallas_call(
        _mix_kernel,
        grid=(m // tm,),
        in_specs=[row(D_MODEL), row(V_WIDTH), row(GM_WIDTH), row(2 * D_MODEL)]
                 + [_resident(w.shape) for w in wts] + [_resident(g.shape) for g in gains],
        out_specs=[row(D_MODEL), row(D_MODEL)],
        out_shape=[jax.ShapeDtypeStruct((m, D_MODEL), F32), jax.ShapeDtypeStruct((m, D_MODEL), BF16)],
        compiler_params=_params(1),
        name="mix",
    )(x, oa, ogm, gates, *wts, *gains)


def _out_kernel(x1_ref, f_ref, pe_ref, wpg_ref, wpe_ref, g_ffn_ref, g_pe_ref, y_ref):
    y_ref[...] = _out_stage(x1_ref[...], f_ref[...], pe_ref[...], wpg_ref[...], wpe_ref[...],
                            g_ffn_ref[...], g_pe_ref[...])


def _out(x1, f, pe, wts, gains, *, tm):
    m = x1.shape[0]
    assert m % tm == 0
    row = lambda w: pl.BlockSpec((tm, w), lambda i: (i, 0))
    return pl.pallas_call(
        _out_kernel,
        grid=(m // tm,),
        in_specs=[row(D_MODEL), row(D_MODEL), row(PLE_DIM)]
                 + [_resident(w.shape) for w in wts] + [_resident(g.shape) for g in gains],
        out_specs=row(D_MODEL),
        out_shape=jax.ShapeDtypeStruct((m, D_MODEL), F32),
        compiler_params=_params(1),
        name="out",
    )(x1, f, pe, *wts, *gains)


def _gmlp_blocks(ws, bs, chunk_len):
    rep = GM_CHUNK // chunk_len
    ws_blk = jnp.tile(ws[:, :chunk_len, :chunk_len], (1, rep, rep))
    bias = jnp.tile(jnp.transpose(bs[:, :chunk_len]), (rep, 1))
    return ws_blk, jnp.repeat(bias, GM_GROUP_DIM, axis=1)


def kernel(x_prompt, x_sample, cache_k, cache_v, page_table, p_prompt, p_sample, pre_mix_g, w_in, lam_q1, lam_k1, lam_q2, lam_k2, subln_g, gm_norm_g, gm_norm_b, gm_ws, gm_bs, w_branch_attn, w_branch_gm, w_o, post_mix_g, pre_ffn_g, w_up, w_down, post_ffn_g, w_pe, w_pg, pe_post_g):
    depth, n_pool = cache_k.shape[0], cache_k.shape[1]
    assert depth == 1
    b, s, _ = x_prompt.shape
    n, t_new, _ = x_sample.shape
    i = 0
    tm = 256
    tm_wide = 512
    lam0 = _lambda_init(i)
    row2 = lambda a: a[i].reshape(1, -1)
    bf = lambda w: w[i].astype(BF16)
    w_in_b = bf(w_in)
    wkt_b = jnp.transpose(w_in[i][:, _K0:_V0]).astype(BF16)
    mix_w = (bf(w_branch_attn), bf(w_branch_gm), bf(w_o))
    mix_g = (row2(post_mix_g), row2(pre_ffn_g))
    w_up_b, w_down_b = bf(w_up), bf(w_down)
    out_w = (bf(w_pg), bf(w_pe))
    out_g = (row2(post_ffn_g), row2(pe_post_g))
    lamv = jnp.stack([lam_q1[i], lam_k1[i], lam_q2[i], lam_k2[i]])
    sg = row2(subln_g)
    norm_args = (row2(gm_norm_g), row2(gm_norm_b))

    xp = x_prompt.reshape(b * s, D_MODEL)
    chunk = min(s, GM_CHUNK)
    qb, kt, ktb, v_rows, vb, gvn_p, ogm, gates = _inproj(
        xp, row2(pre_mix_g), w_in_b, wkt_b, *norm_args, *_gmlp_blocks(gm_ws[i], gm_bs[i], chunk),
        chunk_len=chunk, seq_len=s, tm=tm_wide)
    oa = _prompt_attn(lamv, qb.reshape(b, s, -1), ktb, vb.reshape(b, s, -1), sg, lam0=lam0, tq=256, heads=2)
    x1_p, h_p = _mix(xp, oa.reshape(b * s, -1), ogm, gates, mix_w, mix_g, tm=tm_wide)

    xs = x_sample.reshape(n * t_new, D_MODEL)
    chunk = min(t_new, GM_CHUNK)
    q, k_s, v_s, gvn_s, ogm, gates = _inproj(
        xs, row2(pre_mix_g), w_in_b, None, *norm_args, *_gmlp_blocks(gm_ws[i], gm_bs[i], chunk),
        chunk_len=chunk, seq_len=t_new, tm=tm)
    cache_kt = jnp.transpose(cache_k, (0, 1, 3, 4, 5, 2)).reshape(depth * n_pool, QK_WIDTH, PAGE)
    cache_vr = cache_v.reshape(depth * n_pool, PAGE * N_HEADS, HEAD_DIM)
    oa, yp = _sample_attn_ffn(page_table, lamv, q.reshape(n, t_new, -1), k_s.reshape(n, t_new, -1),
                              v_s.reshape(n, t_new, -1), sg, cache_kt, cache_vr,
                              h_p, w_up_b, w_down_b, x1_p, p_prompt[i].reshape(b * s, -1), out_w, out_g,
                              page0=i * n_pool, lam0=lam0, tm=tm)

    x1_s, h_s = _mix(xs, oa.reshape(n * t_new, -1), ogm, gates, mix_w, mix_g, tm=tm)
    ys = _out(x1_s, _ffn(h_s, w_up_b, w_down_b, tm=tm), p_sample[i].reshape(n * t_new, -1), out_w, out_g, tm=tm)

    k_prompt = jnp.transpose(kt.reshape(1, b, N_HEADS, 2, QK_DIM, s), (0, 1, 5, 2, 3, 4))
    return (yp.reshape(b, s, D_MODEL), ys.reshape(n, t_new, D_MODEL),
            k_prompt, v_rows.reshape(1, b, s, N_HEADS, HEAD_DIM), gvn_p.reshape(1, b, s, GM_WIDTH),
            k_s.reshape(1, n, t_new, N_HEADS, 2, QK_DIM), v_s.reshape(1, n, t_new, N_HEADS, HEAD_DIM),
            gvn_s.reshape(1, n, t_new, GM_WIDTH))
```
